```python
import math
import jax
import jax.numpy as jnp
from jax import lax
import numpy as np

D_MODEL = 1024
BATCH = 8
SEQ = 2048
DEPTH = 4
DEC_BATCH = 128
DEC_SEQ = 4
PAST_LEN = 2048
PAGE_SIZE = 128

NSA_HEADS = 8
NSA_GROUPS = 2
NSA_HPG = NSA_HEADS // NSA_GROUPS
NSA_DH = 64
NSA_WIDTH = NSA_HEADS * NSA_DH
NSA_SCALE = NSA_DH ** -0.5
CMP_LEN = 32
CMP_STRIDE = 16
CMP_RATIO = CMP_LEN // CMP_STRIDE
CMP_HID = 128
SEL_LEN = 64
SEL_TOP = 16
WINDOW = 512
SEL_QBLOCK = 64
WIN_QBLOCK = 128
HG_HEADS = 4
HG_DK = 128
HG_DV = 128
HG_WIDTH = HG_HEADS * HG_DV
HG_CHUNK = 64
REL_BUCKETS = 32
REL_MAX_DIST = 128
D_FF = 2816
N_ADA = 9
EPS = 1e-6
IN_SIZES = (NSA_WIDTH, 6 * NSA_GROUPS * NSA_DH, 3 * NSA_HEADS, HG_HEADS * HG_DK, HG_HEADS * HG_DK, HG_WIDTH, HG_WIDTH, D_MODEL, D_MODEL)
D_IN = NSA_WIDTH + 6 * NSA_GROUPS * NSA_DH + 3 * NSA_HEADS + 2 * HG_HEADS * HG_DK + 2 * HG_WIDTH + 2 * D_MODEL

kernel_name = 'nsa_hgrn2_macaron_adaln_step'


def rmsnorm(x, g):
    xf = x.astype(jnp.float32)
    y = xf * lax.rsqrt(jnp.mean(xf * xf, axis=-1, keepdims=True) + EPS)
    return (y * g.astype(jnp.float32)).astype(x.dtype)


def modulate(x, shift, scale):
    return x * (1 + scale) + shift


def swiglu(x, w_in, w_out):
    a, b = jnp.split(x @ w_in, 2, axis=-1)
    return (jax.nn.silu(a) * b) @ w_out


def rel_bucket(dist):
    n = jnp.maximum(dist, 0)
    exact = REL_BUCKETS // 2
    large = exact + (jnp.log(jnp.maximum(n, 1).astype(jnp.float32) / exact)
                     / math.log(REL_MAX_DIST / exact) * (REL_BUCKETS - exact)).astype(jnp.int32)
    return jnp.where(n < exact, n, jnp.minimum(large, REL_BUCKETS - 1))


def masked_softmax(logits, mask):
    logits = jnp.where(mask, logits, -jnp.inf)
    m = jnp.max(logits, axis=-1, keepdims=True)
    m = jnp.where(jnp.isfinite(m), m, 0.0)
    e = jnp.exp(logits - m)
    s = jnp.sum(e, axis=-1, keepdims=True)
    return e / jnp.where(s > 0, s, 1.0)


def gqa_probs(q, k, mask, bias):
    B, Lq = q.shape[:2]
    qg = q.reshape(B, Lq, NSA_GROUPS, NSA_HPG, NSA_DH)
    logits = jnp.einsum('bqgrd,bkgd->bgrqk', qg, k, preferred_element_type=jnp.float32) * NSA_SCALE
    logits = logits + bias.reshape(NSA_GROUPS, NSA_HPG, Lq, -1).astype(jnp.float32)
    return masked_softmax(logits, mask)


def gqa_apply(p, v):
    B, _, _, Lq, _ = p.shape
    o = jnp.einsum('bgrqk,bkgd->bqgrd', p.astype(v.dtype), v)
    return o.reshape(B, Lq, NSA_HEADS, NSA_DH)


def compress(kv, w1, b1, w2, b2, n_cmp):
    B = kv.shape[0]
    n_sub = n_cmp + CMP_RATIO - 1
    sub = kv[:, :n_sub * CMP_STRIDE].reshape(B, n_sub, CMP_STRIDE, NSA_GROUPS, NSA_DH)
    h = b1
    for r in range(CMP_RATIO):
        h = h + jnp.einsum('bnjgd,jde->bnge', sub[:, r:r + n_cmp], w1[r])
    return jnp.einsum('bnge,ed->bngd', jax.nn.silu(h), w2) + b2


def block_overlap(n_cmp, n_sel):
    c0 = np.arange(n_cmp) * CMP_STRIDE
    s0 = np.arange(n_sel) * SEL_LEN
    return ((c0[:, None] <= s0[None] + SEL_LEN - 1) & (c0[:, None] + CMP_LEN - 1 >= s0[None])).astype(np.float32)


def sel_attend(q, idx, q_pos, ksb, vsb, table_g):
    B, Lq = q.shape[:2]
    n = idx.shape[-1]
    pick = jax.vmap(jax.vmap(lambda blocks, i: blocks[i]))
    kg = pick(ksb, idx).reshape(B, NSA_GROUPS, Lq, n * SEL_LEN, NSA_DH)
    vg = pick(vsb, idx).reshape(B, NSA_GROUPS, Lq, n * SEL_LEN, NSA_DH)
    kpos = (idx[..., None] * SEL_LEN + jnp.arange(SEL_LEN)).reshape(B, NSA_GROUPS, Lq, n * SEL_LEN)
    dist = q_pos[:, None] - kpos
    bias = table_g[jnp.arange(NSA_GROUPS)[None, :, None, None], rel_bucket(dist)]
    qg = q.reshape(B, Lq, NSA_GROUPS, NSA_HPG, NSA_DH)
    logits = jnp.einsum('bqgrd,bgqkd->bgrqk', qg, kg, preferred_element_type=jnp.float32) * NSA_SCALE
    logits = logits + jnp.moveaxis(bias, -1, 2).astype(jnp.float32)
    p = masked_softmax(logits, (dist >= 0)[:, :, None])
    o = jnp.einsum('bgrqk,bgqkd->bqgrd', p.astype(vg.dtype), vg)
    return o.reshape(B, Lq, NSA_HEADS, NSA_DH)


def nsa_mixer(q, kv_new, gates, past_kv, win_buf, w1, b1, w2, b2, rel_table):
    B, L = q.shape[:2]
    q_pos0 = 0 if past_kv is None else past_kv.shape[1]
    q_pos = q_pos0 + jnp.arange(L)
    full = kv_new[:, :, :4] if past_kv is None else jnp.concatenate([past_kv, kv_new[:, :, :4]], axis=1)
    T = full.shape[1]
    n_cmp = (T - CMP_LEN) // CMP_STRIDE + 1
    k_cmp = compress(full[:, :, 0], w1[0], b1[0], w2[0], b2[0], n_cmp)
    v_cmp = compress(full[:, :, 1], w1[1], b1[1], w2[1], b2[1], n_cmp)
    dist_c = q_pos[:, None] - (jnp.arange(n_cmp) * CMP_STRIDE + (CMP_LEN - 1))[None, :]
    p_cmp = gqa_probs(q, k_cmp, dist_c >= 0, jnp.moveaxis(rel_table[rel_bucket(dist_c)], -1, 0))
    o_cmp = gqa_apply(p_cmp, v_cmp)
    n_sel = -(-T // SEL_LEN)
    overlap = jnp.asarray(block_overlap(n_cmp, n_sel), dtype=p_cmp.dtype)
    score = jnp.einsum('bgrqc,cs->bgqs', p_cmp, overlap)
    blk = jnp.arange(n_sel)[None, :]
    cur = (q_pos // SEL_LEN)[:, None]
    forced = (blk == 0) | (blk == cur) | (blk == cur - 1)
    score = jnp.where(forced, jnp.inf, jnp.where(blk <= cur, score, -jnp.inf))
    _, idx = lax.top_k(score, min(SEL_TOP, n_sel))
    pad = n_sel * SEL_LEN - T

    def to_blocks(a):
        a = jnp.pad(a, ((0, 0), (0, pad), (0, 0), (0, 0)))
        return a.reshape(B, n_sel, SEL_LEN, NSA_GROUPS, NSA_DH).transpose(0, 3, 1, 2, 4)

    ksb, vsb = to_blocks(full[:, :, 2]), to_blocks(full[:, :, 3])
    table_g = rel_table.reshape(REL_BUCKETS, NSA_GROUPS, NSA_HPG).transpose(1, 0, 2)
    if L > SEL_QBLOCK and L % SEL_QBLOCK == 0:
        nb = L // SEL_QBLOCK
        qs = q.reshape(B, nb, SEL_QBLOCK, NSA_HEADS, NSA_DH).transpose(1, 0, 2, 3, 4)
        ids = idx.reshape(B, NSA_GROUPS, nb, SEL_QBLOCK, -1).transpose(2, 0, 1, 3, 4)
        ps = q_pos.reshape(nb, SEL_QBLOCK)

        def sel_block(args):
            qb, ib, pb = args
            return sel_attend(qb, ib, pb, ksb, vsb, table_g)

        o_sel = lax.map(sel_block, (qs, ids, ps)).transpose(1, 0, 2, 3, 4).reshape(B, L, NSA_HEADS, NSA_DH)
    else:
        o_sel = sel_attend(q, idx, q_pos, ksb, vsb, table_g)
    if win_buf is None:
        nb = L // WIN_QBLOCK
        kw_len = WIN_QBLOCK + WINDOW
        rows = (jnp.arange(nb) * WIN_QBLOCK)[:, None] + jnp.arange(kw_len)[None, :]
        kpos = rows - WINDOW
        wkv = jnp.pad(kv_new[:, :, 4:], ((0, 0), (WINDOW, 0), (0, 0), (0, 0), (0, 0)))[:, rows]
        dist = q_pos.reshape(nb, WIN_QBLOCK)[:, :, None] - kpos[:, None, :]
        mask = (dist >= 0) & (dist < WINDOW) & (kpos[:, None, :] >= 0)
        bias = jnp.moveaxis(rel_table[rel_bucket(dist)], -1, 1)
        qb = q.reshape(B, nb, WIN_QBLOCK, NSA_HEADS, NSA_DH)
        band = jax.vmap(lambda qq, kk, vv, mm, bb: gqa_apply(gqa_probs(qq, kk, mm, bb), vv),
                        in_axes=(1, 1, 1, 0, 0), out_axes=1)
        o_win = band(qb, wkv[:, :, :, 0], wkv[:, :, :, 1], mask, bias).reshape(B, L, NSA_HEADS, NSA_DH)
        new_win = kv_new[:, -min(WINDOW, L):, 4:]
    else:
        wkv = jnp.concatenate([win_buf, kv_new[:, :, 4:]], axis=1)
        kpos = q_pos0 - win_buf.shape[1] + jnp.arange(wkv.shape[1])
        dist = q_pos[:, None] - kpos[None, :]
        mask = (dist >= 0) & (dist < WINDOW)
        p_win = gqa_probs(q, wkv[:, :, 0], mask, jnp.moveaxis(rel_table[rel_bucket(dist)], -1, 0))
        o_win = gqa_apply(p_win, wkv[:, :, 1])
        new_win = wkv[:, -win_buf.shape[1]:]
    o = gates[:, :, 0, :, None] * o_cmp + gates[:, :, 1, :, None] * o_sel + gates[:, :, 2, :, None] * o_win
    return o.reshape(B, L, NSA_WIDTH), new_win


def hgrn_scan(q, k, v, logf, s0):
    B, L, H, _ = q.shape
    DV = v.shape[-1]
    C = math.gcd(L, HG_CHUNK)
    nc = L // C

    def chunks(a):
        return a.astype(jnp.float32).reshape(B, nc, C, H, a.shape[-1]).transpose(1, 0, 3, 2, 4)

    causal = jnp.tril(jnp.ones((C, C), dtype=bool))[:, :, None]

    def step(s, inp):
        qc, kc, vc, lc = inp
        b = jnp.cumsum(lc, axis=2)
        decay = jnp.exp(jnp.where(causal, b[:, :, :, None, :] - b[:, :, None, :, :], -jnp.inf))
        a = jnp.einsum('bhtk,bhsk,bhtsk->bhts', qc, kc, decay)
        o = jnp.einsum('bhtk,bhkv->bhtv', qc * jnp.exp(b), s) + jnp.einsum('bhts,bhsv->bhtv', a, vc)
        b_last = b[:, :, -1:, :]
        s = jnp.exp(b_last[:, :, 0, :, None]) * s + jnp.einsum('bhsk,bhsv->bhkv', kc * jnp.exp(b_last - b), vc)
        return s, o

    s, o = lax.scan(step, s0.astype(jnp.float32), (chunks(q), chunks(k), chunks(v), chunks(logf)))
    return o.transpose(1, 0, 3, 2, 4).reshape(B, L, H, DV), s


def hgrn2_mixer(q_raw, f_raw, i_raw, g_raw, lb, s0, gain):
    B, L = q_raw.shape[:2]
    q = jax.nn.silu(q_raw.reshape(B, L, HG_HEADS, HG_DK))
    fz = f_raw.reshape(B, L, HG_HEADS, HG_DK).astype(jnp.float32)
    logf = jnp.log(lb + (1 - lb) * jax.nn.sigmoid(fz))
    k = (1 - lb) * jax.nn.sigmoid(-fz)
    o, s = hgrn_scan(q, k, i_raw.reshape(B, L, HG_HEADS, HG_DV), logf, s0)
    o = rmsnorm(o.astype(q_raw.dtype), gain) * jax.nn.silu(g_raw.reshape(B, L, HG_HEADS, HG_DV))
    return o.reshape(B, L, HG_WIDTH), s


def setup_inputs(seed: int = 0) -> dict:
    key = jax.random.key(seed)
    ks = jax.random.split(key, 25)
    f32 = jnp.float32

    def nrm(k, shape, scale):
        return jax.random.normal(k, shape, f32) * scale

    n_pages = PAST_LEN // PAGE_SIZE
    n_used = DEC_BATCH * n_pages
    n_pool = n_used + (n_used + 3) // 4
    page_table = jax.random.permutation(ks[0], n_pool)[:n_used].reshape(DEC_BATCH, n_pages).astype(jnp.int32)
    w_buf = min(WINDOW, PAST_LEN)
    return {
        'x_prompt': nrm(ks[1], (BATCH, SEQ, D_MODEL), 1.0),
        'x_sample': nrm(ks[2], (DEC_BATCH, DEC_SEQ, D_MODEL), 1.0),
        'cache_nsa_kv': nrm(ks[3], (DEPTH, n_pool, PAGE_SIZE, 4, NSA_GROUPS, NSA_DH), 1.0),
        'state_win_kv': nrm(ks[4], (DEPTH, DEC_BATCH, w_buf, 2, NSA_GROUPS, NSA_DH), 1.0),
        'state_hgrn': nrm(ks[5], (DEPTH, DEC_BATCH, HG_HEADS, HG_DK, HG_DV), 0.5),
        'page_table': page_table,
        'c_prompt': nrm(ks[6], (BATCH, D_MODEL), 1.0),
        'c_sample': nrm(ks[7], (DEC_BATCH, D_MODEL), 1.0),
        'norm_g': 1.0 + nrm(ks[8], (DEPTH, 3, D_MODEL), 0.02),
        'ada_w': nrm(ks[9], (DEPTH, D_MODEL, N_ADA * D_MODEL), 0.5 * D_MODEL ** -0.5),
        'ada_b': nrm(ks[10], (DEPTH, N_ADA * D_MODEL), 0.02),
        'ffn_w_in': nrm(ks[11], (DEPTH, 2, D_MODEL, 2 * D_FF), D_MODEL ** -0.5),
        'ffn_w_out': nrm(ks[12], (DEPTH, 2, D_FF, D_MODEL), D_FF ** -0.5),
        'w_in': nrm(ks[13], (DEPTH, D_MODEL, D_IN), D_MODEL ** -0.5),
        'cmp_w1': nrm(ks[14], (DEPTH, 2, CMP_RATIO, CMP_STRIDE, NSA_DH, CMP_HID), (CMP_LEN * NSA_DH) ** -0.5),
        'cmp_b1': nrm(ks[15], (DEPTH, 2, CMP_HID), 0.02),
        'cmp_w2': nrm(ks[16], (DEPTH, 2, CMP_HID, NSA_DH), CMP_HID ** -0.5),
        'cmp_b2': nrm(ks[17], (DEPTH, 2, NSA_DH), 0.02),
        'rel_table': nrm(ks[18], (REL_BUCKETS, NSA_HEADS), 0.5),
        'hgrn_lb_logits': nrm(ks[19], (DEPTH, HG_HEADS * HG_DK), 1.0),
        'hgrn_norm_g': 1.0 + nrm(ks[20], (DEPTH, HG_DV), 0.02),
        'w_branch_nsa': nrm(ks[21], (DEPTH, NSA_WIDTH, D_MODEL), NSA_WIDTH ** -0.5),
        'w_branch_hgrn': nrm(ks[22], (DEPTH, HG_WIDTH, D_MODEL), HG_WIDTH ** -0.5),
        'w_out': nrm(ks[23], (DEPTH, D_MODEL, D_MODEL), D_MODEL ** -0.5),
        'final_g': 1.0 + nrm(ks[24], (D_MODEL,), 0.02),
    }


def reference(x_prompt, x_sample, cache_nsa_kv, state_win_kv, state_hgrn, page_table,
              c_prompt, c_sample, norm_g, ada_w, ada_b, ffn_w_in, ffn_w_out, w_in,
              cmp_w1, cmp_b1, cmp_w2, cmp_b2, rel_table, hgrn_lb_logits, hgrn_norm_g,
              w_branch_nsa, w_branch_hgrn, w_out, final_g):
    lb = jnp.cumsum(jax.nn.softmax(hgrn_lb_logits.astype(jnp.float32), axis=0), axis=0)
    lb = (lb - lb[:1]).reshape(DEPTH, HG_HEADS, HG_DK)
    splits = [int(s) for s in np.cumsum(IN_SIZES)[:-1]]

    def trunk(x, c, past_fn):
        B, L, _ = x.shape
        kv_rows, win_bufs, hg_states = [], [], []
        for l in range(DEPTH):
            past_kv, win_buf, s0 = past_fn(l)
            mods = (jax.nn.silu(c) @ ada_w[l] + ada_b[l]).reshape(B, N_ADA, 1, D_MODEL)
            shift, scale, gate = mods[:, 0::3], mods[:, 1::3], mods[:, 2::3]
            n = modulate(rmsnorm(x, norm_g[l, 0]), shift[:, 0], scale[:, 0])
            x = x + 0.5 * gate[:, 0] * swiglu(n, ffn_w_in[l, 0], ffn_w_out[l, 0])
            n = modulate(rmsnorm(x, norm_g[l, 1]), shift[:, 1], scale[:, 1])
            qa, kva, ga, qb, fb, ib, gb, ma, mb = jnp.split(n @ w_in[l], splits, axis=-1)
            kv_new = kva.reshape(B, L, 6, NSA_GROUPS, NSA_DH)
            o_a, new_win = nsa_mixer(qa.reshape(B, L, NSA_HEADS, NSA_DH), kv_new,
                                     jax.nn.sigmoid(ga.reshape(B, L, 3, NSA_HEADS)), past_kv, win_buf,
                                     cmp_w1[l], cmp_b1[l], cmp_w2[l], cmp_b2[l], rel_table)
            o_b, s_new = hgrn2_mixer(qb, fb, ib, gb, lb[l], s0, hgrn_norm_g[l])
            merged = jax.nn.sigmoid(ma) * (o_a @ w_branch_nsa[l]) + jax.nn.sigmoid(mb) * (o_b @ w_branch_hgrn[l])
            x = x + gate[:, 1] * (merged @ w_out[l])
            n = modulate(rmsnorm(x, norm_g[l, 2]), shift[:, 2], scale[:, 2])
            x = x + 0.5 * gate[:, 2] * swiglu(n, ffn_w_in[l, 1], ffn_w_out[l, 1])
            kv_rows.append(kv_new[:, :, :4])
            win_bufs.append(new_win)
            hg_states.append(s_new)
        return rmsnorm(x, final_g), jnp.stack(kv_rows), jnp.stack(win_bufs), jnp.stack(hg_states)

    n_seq, n_pages = page_table.shape

    def prompt_past(l):
        return None, None, jnp.zeros((x_prompt.shape[0], HG_HEADS, HG_DK, HG_DV), jnp.float32)

    def sample_past(l):
        past = cache_nsa_kv[l][page_table].reshape(n_seq, n_pages * PAGE_SIZE, 4, NSA_GROUPS, NSA_DH)
        return past, state_win_kv[l], state_hgrn[l]

    y_prompt, kv_p, win_p, hg_p = trunk(x_prompt, c_prompt, prompt_past)
    y_sample, kv_s, win_s, hg_s = trunk(x_sample, c_sample, sample_past)
    return (y_prompt, y_sample, kv_p, kv_s, win_p, win_s, hg_p, hg_s)
```

```python
import functools
import math

import jax
import jax.numpy as jnp
import numpy as np
from jax import lax
from jax.experimental import pallas as pl
from jax.experimental.pallas import tpu as pltpu

NSA_HEADS = 8
NSA_GROUPS = 2
NSA_HPG = NSA_HEADS // NSA_GROUPS
NSA_DH = 64
NSA_WIDTH = NSA_HEADS * NSA_DH
NSA_SCALE = NSA_DH ** -0.5
CMP_LEN = 32
CMP_STRIDE = 16
CMP_RATIO = CMP_LEN // CMP_STRIDE
CMP_HID = 128
SEL_LEN = 64
SEL_TOP = 16
WINDOW = 512
SEL_QBLOCK = 64
WIN_QBLOCK = 128
HG_HEADS = 4
HG_DK = 128
HG_DV = 128
HG_WIDTH = HG_HEADS * HG_DV
HG_CHUNK = 64
REL_BUCKETS = 32
REL_MAX_DIST = 128
N_ADA = 9
EPS = 1e-6

BF16 = jnp.bfloat16
F32 = jnp.float32

VMEM_LIMIT_BYTES = 56 * 1024 * 1024

P_MA, P_MB, P_QA, P_KV4, P_WING, P_QB, P_FB, P_IB, P_GB, P_TOTAL = (
    0, 1024, 2048, 2560, 3072, 3584, 4096, 4608, 5120, 5632)


def _proj_column_index(d_model):
    sizes = (NSA_WIDTH, 6 * NSA_GROUPS * NSA_DH, 3 * NSA_HEADS, HG_HEADS * HG_DK, HG_HEADS * HG_DK,
             HG_WIDTH, HG_WIDTH, d_model, d_model)
    offs = np.concatenate([[0], np.cumsum(sizes)])
    d_in = int(offs[-1])
    qa, kva, ga, qb, fb, ib, gb, ma, mb = [np.arange(offs[i], offs[i + 1]) for i in range(9)]
    pad = np.full((512 - 256 - ga.size,), d_in)
    idx = np.concatenate([ma, mb, qa, kva[:512], kva[512:], ga, pad, qb, fb, ib, gb])
    assert idx.size == P_TOTAL and d_model == 1024
    return idx.astype(np.int32), d_in


def _mods_kernel(c_ref, w_ref, b_ref, o_ref):
    c = c_ref[...]
    sc = (c * jax.nn.sigmoid(c)).astype(BF16)
    o_ref[...] = jnp.dot(sc, w_ref[...].astype(BF16), preferred_element_type=F32) + b_ref[...]


def _mods_call(c_all, ada_w, ada_b):
    depth, d, nd = ada_w.shape
    n = c_all.shape[0]
    b3 = ada_b.reshape(depth, 1, nd)
    return pl.pallas_call(
        _mods_kernel,
        grid=(depth, nd // d),
        in_specs=[
            pl.BlockSpec((n, d), lambda l, j: (0, 0)),
            pl.BlockSpec((None, d, d), lambda l, j: (l, 0, j)),
            pl.BlockSpec((None, 1, d), lambda l, j: (l, 0, j)),
        ],
        out_specs=pl.BlockSpec((None, None, n, d), lambda l, j: (l, j, 0, 0)),
        out_shape=jax.ShapeDtypeStruct((depth, nd // d, n, d), F32),
        compiler_params=pltpu.CompilerParams(dimension_semantics=("arbitrary", "arbitrary")),
        name="adaln_mods",
    )(c_all, ada_w, b3)


def _rows(m, tm):
    r = m.shape[0]
    if r == 1 or r == tm:
        return m
    return jnp.concatenate([m] * (tm // r), axis=0)


def _norm_mod(x, g, shift, scale):
    y = x * lax.rsqrt(jnp.mean(x * x, axis=-1, keepdims=True) + EPS) * g
    return y * (1.0 + scale) + shift


def _mod_spec(mods, l, k, tm, seq_tiles):
    if seq_tiles is not None:
        row0, tiles_per_seq = seq_tiles
        d = mods.shape[-1]
        m5 = mods.reshape(mods.shape[0], mods.shape[1], mods.shape[2], 1, d)
        return m5, pl.BlockSpec((None, None, None, 1, d),
                                lambda i, j, l=l, k=k: (l, k, row0 + i // tiles_per_seq, 0, 0))
    d = mods.shape[-1]
    return mods, pl.BlockSpec((None, None, 128, d), lambda i, j, l=l, k=k: (l, k, 0, 0))


def _ffn_kernel(x_ref, g_ref, sh_ref, sc_ref, gt_ref, wa_ref, wb_ref, wo_ref, fg_ref, o_ref,
                n_scr, acc_scr, *, nf, final):
    j = pl.program_id(1)
    tm = x_ref.shape[0]

    @pl.when(j == 0)
    def _():
        n = _norm_mod(x_ref[...], g_ref[...], _rows(sh_ref[...], tm), _rows(sc_ref[...], tm))
        n_scr[...] = n.astype(BF16)
        acc_scr[...] = jnp.zeros_like(acc_scr)

    n = n_scr[...]
    a = jnp.dot(n, wa_ref[...], preferred_element_type=F32)
    b = jnp.dot(n, wb_ref[...], preferred_element_type=F32)
    h = (a * jax.nn.sigmoid(a) * b).astype(BF16)
    acc_scr[...] += jnp.dot(h, wo_ref[...], preferred_element_type=F32)

    @pl.when(j == nf - 1)
    def _():
        y = x_ref[...] + 0.5 * _rows(gt_ref[...], tm) * acc_scr[...]
        if final:
            y = y * lax.rsqrt(jnp.mean(y * y, axis=-1, keepdims=True) + EPS) * fg_ref[...]
        o_ref[...] = y


def _ffn_call(x, norm_g, mods, ffn_w_in, ffn_w_out, final_g, *, l, s, sub, tm, tf, seq_tiles, final):
    t, d = x.shape
    d_ff = ffn_w_out.shape[2]
    nf = d_ff // tf
    g2 = norm_g[l, sub].reshape(1, d)
    fg2 = final_g.reshape(1, d)
    m_sh, sp_sh = _mod_spec(mods, l, 3 * sub, tm, seq_tiles)
    m_sc, sp_sc = _mod_spec(mods, l, 3 * sub + 1, tm, seq_tiles)
    m_gt, sp_gt = _mod_spec(mods, l, 3 * sub + 2, tm, seq_tiles)
    return pl.pallas_call(
        functools.partial(_ffn_kernel, nf=nf, final=final),
        grid=(t // tm, nf),
        in_specs=[
            pl.BlockSpec((tm, d), lambda i, j: (i, 0)),
            pl.BlockSpec((1, d), lambda i, j: (0, 0)),
            sp_sh, sp_sc, sp_gt,
            pl.BlockSpec((None, None, d, tf), lambda i, j: (l, s, 0, j)),
            pl.BlockSpec((None, None, d, tf), lambda i, j: (l, s, 0, nf + j)),
            pl.BlockSpec((None, None, tf, d), lambda i, j: (l, s, j, 0)),
            pl.BlockSpec((1, d), lambda i, j: (0, 0)),
        ],
        out_specs=pl.BlockSpec((tm, d), lambda i, j: (i, 0)),
        out_shape=jax.ShapeDtypeStruct((t, d), F32),
        scratch_shapes=[pltpu.VMEM((tm, d), BF16), pltpu.VMEM((tm, d), F32)],
        compiler_params=pltpu.CompilerParams(
            dimension_semantics=("parallel", "arbitrary"), vmem_limit_bytes=VMEM_LIMIT_BYTES),
        name="ffn",
    )(x, g2, m_sh, m_sc, m_gt, ffn_w_in, ffn_w_in, ffn_w_out, fg2)


def _proj_kernel(x_ref, g_ref, sh_ref, sc_ref, w_ref, o_ref, n_scr):
    j = pl.program_id(1)
    tm = x_ref.shape[0]

    @pl.when(j == 0)
    def _():
        n = _norm_mod(x_ref[...], g_ref[...], _rows(sh_ref[...], tm), _rows(sc_ref[...], tm))
        n_scr[...] = n.astype(BF16)

    o_ref[...] = jnp.dot(n_scr[...], w_ref[...], preferred_element_type=F32)


def _proj_call(x, norm_g, mods, w_in_p, *, l, tm, tn, seq_tiles):
    t, d = x.shape
    n_out = w_in_p.shape[2]
    g2 = norm_g[l, 1].reshape(1, d)
    m_sh, sp_sh = _mod_spec(mods, l, 3, tm, seq_tiles)
    m_sc, sp_sc = _mod_spec(mods, l, 4, tm, seq_tiles)
    return pl.pallas_call(
        _proj_kernel,
        grid=(t // tm, n_out // tn),
        in_specs=[
            pl.BlockSpec((tm, d), lambda i, j: (i, 0)),
            pl.BlockSpec((1, d), lambda i, j: (0, 0)),
            sp_sh, sp_sc,
            pl.BlockSpec((None, d, tn), lambda i, j: (l, 0, j)),
        ],
        out_specs=pl.BlockSpec((tm, tn), lambda i, j: (i, j)),
        out_shape=jax.ShapeDtypeStruct((t, n_out), F32),
        scratch_shapes=[pltpu.VMEM((tm, d), BF16)],
        compiler_params=pltpu.CompilerParams(
            dimension_semantics=("parallel", "arbitrary"), vmem_limit_bytes=VMEM_LIMIT_BYTES),
        name="in_proj",
    )(x, g2, m_sh, m_sc, w_in_p)


def _merge_kernel(x_ref, oa_ref, ob_ref, ma_ref, mb_ref, gt_ref, wpa_ref, wpb_ref, wo_ref, o_ref):
    tm = x_ref.shape[0]
    ya = jnp.dot(oa_ref[...].astype(BF16), wpa_ref[...], preferred_element_type=F32)
    yb = jnp.dot(ob_ref[...].astype(BF16), wpb_ref[...], preferred_element_type=F32)
    merged = jax.nn.sigmoid(ma_ref[...]) * ya + jax.nn.sigmoid(mb_ref[...]) * yb
    y = jnp.dot(merged.astype(BF16), wo_ref[...], preferred_element_type=F32)
    o_ref[...] = x_ref[...] + _rows(gt_ref[...], tm) * y


def _merge_call(x, o_a, o_b, p, mods, wpa, wpb, wo, *, l, tm, seq_tiles):
    t, d = x.shape
    m_gt, sp_gt = _mod_spec(mods, l, 5, tm, seq_tiles)
    one = lambda i, j: (i, 0)
    return pl.pallas_call(
        _merge_kernel,
        grid=(t // tm, 1),
        in_specs=[
            pl.BlockSpec((tm, d), one),
            pl.BlockSpec((tm, NSA_WIDTH), one),
            pl.BlockSpec((tm, HG_WIDTH), one),
            pl.BlockSpec((tm, d), lambda i, j: (i, P_MA // d)),
            pl.BlockSpec((tm, d), lambda i, j: (i, P_MB // d)),
            sp_gt,
            pl.BlockSpec((None, NSA_WIDTH, d), lambda i, j: (l, 0, 0)),
            pl.BlockSpec((None, HG_WIDTH, d), lambda i, j: (l, 0, 0)),
            pl.BlockSpec((None, d, d), lambda i, j: (l, 0, 0)),
        ],
        out_specs=pl.BlockSpec((tm, d), one),
        out_shape=jax.ShapeDtypeStruct((t, d), F32),
        compiler_params=pltpu.CompilerParams(
            dimension_semantics=("parallel", "arbitrary"), vmem_limit_bytes=VMEM_LIMIT_BYTES),
        name="merge_out",
    )(x, o_a, o_b, p, p, m_gt, wpa, wpb, wo)


def _rmsnorm(x, g):
    xf = x.astype(F32)
    y = xf * lax.rsqrt(jnp.mean(xf * xf, axis=-1, keepdims=True) + EPS)
    return (y * g.astype(F32)).astype(x.dtype)


def _rel_bucket(dist):
    n = jnp.maximum(dist, 0)
    exact = REL_BUCKETS // 2
    large = exact + (jnp.log(jnp.maximum(n, 1).astype(F32) / exact)
                     / math.log(REL_MAX_DIST / exact) * (REL_BUCKETS - exact)).astype(jnp.int32)
    return jnp.where(n < exact, n, jnp.minimum(large, REL_BUCKETS - 1))


def _masked_softmax(logits, mask):
    logits = jnp.where(mask, logits, -jnp.inf)
    m = jnp.max(logits, axis=-1, keepdims=True)
    m = jnp.where(jnp.isfinite(m), m, 0.0)
    e = jnp.exp(logits - m)
    s = jnp.sum(e, axis=-1, keepdims=True)
    return e / jnp.where(s > 0, s, 1.0)


def _gqa_probs(q, k, mask, bias):
    B, Lq = q.shape[:2]
    qg = q.reshape(B, Lq, NSA_GROUPS, NSA_HPG, NSA_DH)
    logits = jnp.einsum('bqgrd,bkgd->bgrqk', qg, k, preferred_element_type=F32) * NSA_SCALE
    logits = logits + bias.reshape(NSA_GROUPS, NSA_HPG, Lq, -1).astype(F32)
    return _masked_softmax(logits, mask)


def _gqa_apply(p, v):
    B, _, _, Lq, _ = p.shape
    o = jnp.einsum('bgrqk,bkgd->bqgrd', p.astype(v.dtype), v)
    return o.reshape(B, Lq, NSA_HEADS, NSA_DH)


def _compress(kv, w1, b1, w2, b2, n_cmp):
    B = kv.shape[0]
    n_sub = n_cmp + CMP_RATIO - 1
    sub = kv[:, :n_sub * CMP_STRIDE].reshape(B, n_sub, CMP_STRIDE, NSA_GROUPS, NSA_DH)
    h = b1
    for r in range(CMP_RATIO):
        h = h + jnp.einsum('bnjgd,jde->bnge', sub[:, r:r + n_cmp], w1[r])
    return jnp.einsum('bnge,ed->bngd', jax.nn.silu(h), w2) + b2


def _block_overlap(n_cmp, n_sel):
    c0 = np.arange(n_cmp) * CMP_STRIDE
    s0 = np.arange(n_sel) * SEL_LEN
    return ((c0[:, None] <= s0[None] + SEL_LEN - 1) & (c0[:, None] + CMP_LEN - 1 >= s0[None])).astype(np.float32)


def _nsa_mixer(q, kv_new, gates, past_kv, win_buf, w1, b1, w2, b2, rel_table):
    B, L = q.shape[:2]
    q_pos0 = 0 if past_kv is None else past_kv.shape[1]
    q_pos = q_pos0 + jnp.arange(L)
    full = kv_new[:, :, :4] if past_kv is None else jnp.concatenate([past_kv, kv_new[:, :, :4]], axis=1)
    T = full.shape[1]
    n_cmp = (T - CMP_LEN) // CMP_STRIDE + 1
    k_cmp = _compress(full[:, :, 0], w1[0], b1[0], w2[0], b2[0], n_cmp)
    v_cmp = _compress(full[:, :, 1], w1[1], b1[1], w2[1], b2[1], n_cmp)
    dist_c = q_pos[:, None] - (jnp.arange(n_cmp) * CMP_STRIDE + (CMP_LEN - 1))[None, :]
    p_cmp = _gqa_probs(q, k_cmp, dist_c >= 0, jnp.moveaxis(rel_table[_rel_bucket(dist_c)], -1, 0))
    o_cmp = _gqa_apply(p_cmp, v_cmp)
    n_sel = -(-T // SEL_LEN)
    overlap = jnp.asarray(_block_overlap(n_cmp, n_sel), dtype=p_cmp.dtype)
    score = jnp.einsum('bgrqc,cs->bgqs', p_cmp, overlap)
    blk = jnp.arange(n_sel)[None, :]
    cur = (q_pos // SEL_LEN)[:, None]
    forced = (blk == 0) | (blk == cur) | (blk == cur - 1)
    score = jnp.where(forced, jnp.inf, jnp.where(blk <= cur, score, -jnp.inf))
    n_top = min(SEL_TOP, n_sel)
    gt = score[..., :, None] > score[..., None, :]
    eq = (score[..., :, None] == score[..., None, :]) & (blk[0][:, None] < blk[0][None, :])
    rank = jnp.sum((gt | eq).astype(jnp.int32), axis=-2)
    member = rank < n_top
    pad = n_sel * SEL_LEN - T
    ks = jnp.pad(full[:, :, 2], ((0, 0), (0, pad), (0, 0), (0, 0)))
    vs = jnp.pad(full[:, :, 3], ((0, 0), (0, pad), (0, 0), (0, 0)))
    kpos = jnp.arange(n_sel * SEL_LEN)
    dist_s = q_pos[:, None] - kpos[None, :]
    mask_s = jnp.repeat(member, SEL_LEN, axis=-1) & (dist_s >= 0)[None, None]
    bias_s = jnp.moveaxis(rel_table[_rel_bucket(dist_s)], -1, 0)
    qg = q.reshape(B, L, NSA_GROUPS, NSA_HPG, NSA_DH)
    logits = jnp.einsum('bqgrd,bkgd->bgrqk', qg, ks, preferred_element_type=F32) * NSA_SCALE
    logits = logits + bias_s.reshape(NSA_GROUPS, NSA_HPG, L, -1)
    p_sel = _masked_softmax(logits, mask_s[:, :, None])
    o_sel = _gqa_apply(p_sel, vs)
    if win_buf is None:
        kpos_w = jnp.arange(L)
        dist = q_pos[:, None] - kpos_w[None, :]
        mask = (dist >= 0) & (dist < WINDOW)
        p_win = _gqa_probs(q, kv_new[:, :, 4], mask, jnp.moveaxis(rel_table[_rel_bucket(dist)], -1, 0))
        o_win = _gqa_apply(p_win, kv_new[:, :, 5])
        new_win = kv_new[:, -min(WINDOW, L):, 4:]
    else:
        wkv = jnp.concatenate([win_buf, kv_new[:, :, 4:]], axis=1)
        kpos_w = q_pos0 - win_buf.shape[1] + jnp.arange(wkv.shape[1])
        dist = q_pos[:, None] - kpos_w[None, :]
        mask = (dist >= 0) & (dist < WINDOW)
        p_win = _gqa_probs(q, wkv[:, :, 0], mask, jnp.moveaxis(rel_table[_rel_bucket(dist)], -1, 0))
        o_win = _gqa_apply(p_win, wkv[:, :, 1])
        new_win = wkv[:, -win_buf.shape[1]:]
    o = gates[:, :, 0, :, None] * o_cmp + gates[:, :, 1, :, None] * o_sel + gates[:, :, 2, :, None] * o_win
    return o.reshape(B, L, NSA_WIDTH), new_win


def _hgrn_scan(q, k, v, logf, s0):
    B, L, H, _ = q.shape
    DV = v.shape[-1]
    C = math.gcd(L, HG_CHUNK)
    nc = L // C

    def chunks(a):
        return a.astype(F32).reshape(B, nc, C, H, a.shape[-1]).transpose(1, 0, 3, 2, 4)

    causal = jnp.tril(jnp.ones((C, C), dtype=bool))[:, :, None]

    def step(s, inp):
        qc, kc, vc, lc = inp
        b = jnp.cumsum(lc, axis=2)
        decay = jnp.exp(jnp.where(causal, b[:, :, :, None, :] - b[:, :, None, :, :], -jnp.inf))
        a = jnp.einsum('bhtk,bhsk,bhtsk->bhts', qc, kc, decay)
        o = jnp.einsum('bhtk,bhkv->bhtv', qc * jnp.exp(b), s) + jnp.einsum('bhts,bhsv->bhtv', a, vc)
        b_last = b[:, :, -1:, :]
        s = jnp.exp(b_last[:, :, 0, :, None]) * s + jnp.einsum('bhsk,bhsv->bhkv', kc * jnp.exp(b_last - b), vc)
        return s, o

    s, o = lax.scan(step, s0.astype(F32), (chunks(q), chunks(k), chunks(v), chunks(logf)))
    return o.transpose(1, 0, 3, 2, 4).reshape(B, L, H, DV), s


def _hgrn2_mixer(q_raw, f_raw, i_raw, g_raw, lb, s0, gain):
    B, L = q_raw.shape[:2]
    q = jax.nn.silu(q_raw.reshape(B, L, HG_HEADS, HG_DK))
    fz = f_raw.reshape(B, L, HG_HEADS, HG_DK).astype(F32)
    logf = jnp.log(lb + (1 - lb) * jax.nn.sigmoid(fz))
    k = (1 - lb) * jax.nn.sigmoid(-fz)
    o, s = _hgrn_scan(q, k, i_raw.reshape(B, L, HG_HEADS, HG_DV), logf, s0)
    o = _rmsnorm(o.astype(q_raw.dtype), gain) * jax.nn.silu(g_raw.reshape(B, L, HG_HEADS, HG_DV))
    return o.reshape(B, L, HG_WIDTH), s


def kernel(x_prompt, x_sample, cache_nsa_kv, state_win_kv, state_hgrn, page_table, c_prompt, c_sample,
           norm_g, ada_w, ada_b, ffn_w_in, ffn_w_out, w_in, cmp_w1, cmp_b1, cmp_w2, cmp_b2, rel_table,
           hgrn_lb_logits, hgrn_norm_g, w_branch_nsa, w_branch_hgrn, w_out, final_g):
    depth = norm_g.shape[0]
    bp, lp, d = x_prompt.shape
    bs, ls, _ = x_sample.shape
    assert bs == 128, "sample rows are tiled per 128 sequences"

    lb = jnp.cumsum(jax.nn.softmax(hgrn_lb_logits.astype(F32), axis=0), axis=0)
    lb = (lb - lb[:1]).reshape(depth, HG_HEADS, HG_DK)

    col_idx, d_in = _proj_column_index(d)
    w_in_p = jnp.take(jnp.pad(w_in, ((0, 0), (0, 0), (0, 1))), col_idx, axis=2).astype(BF16)
    ffn_w_in_b = ffn_w_in.astype(BF16)
    ffn_w_out_b = ffn_w_out.astype(BF16)
    wpa_b = w_branch_nsa.astype(BF16)
    wpb_b = w_branch_hgrn.astype(BF16)
    wo_b = w_out.astype(BF16)

    mods = _mods_call(jnp.concatenate([c_sample, c_prompt], axis=0), ada_w, ada_b)

    n_seq, n_pages = page_table.shape
    page = cache_nsa_kv.shape[2]

    def trunk(x2, *, nb, ln, tm, seq_tiles, token_major, past_fn):
        kv_rows, win_bufs, hg_states = [], [], []

        def to_bl(a2):
            if token_major:
                return a2.reshape(ln, nb, a2.shape[-1]).transpose(1, 0, 2)
            return a2.reshape(nb, ln, a2.shape[-1])

        def to_t(a3):
            if token_major:
                return a3.transpose(1, 0, 2).reshape(ln * nb, a3.shape[-1])
            return a3.reshape(nb * ln, a3.shape[-1])

        for l in range(depth):
            past_kv, win_buf, s0 = past_fn(l)
            x2 = _ffn_call(x2, norm_g, mods, ffn_w_in_b, ffn_w_out_b, final_g, l=l, s=0, sub=0, tm=tm, tf=256,
                           seq_tiles=seq_tiles, final=False)
            p = _proj_call(x2, norm_g, mods, w_in_p, l=l, tm=tm, tn=512, seq_tiles=seq_tiles)
            p3 = to_bl(p)
            qa = p3[..., P_QA:P_QA + 512]
            kva = p3[..., P_KV4:P_KV4 + 768]
            ga = p3[..., P_WING + 256:P_WING + 256 + 3 * NSA_HEADS]
            qb, fb, ib, gb = (p3[..., o:o + 512] for o in (P_QB, P_FB, P_IB, P_GB))
            kv_new = kva.reshape(nb, ln, 6, NSA_GROUPS, NSA_DH)
            o_a, new_win = _nsa_mixer(qa.reshape(nb, ln, NSA_HEADS, NSA_DH), kv_new,
                                      jax.nn.sigmoid(ga.reshape(nb, ln, 3, NSA_HEADS)), past_kv, win_buf,
                                      cmp_w1[l], cmp_b1[l], cmp_w2[l], cmp_b2[l], rel_table)
            o_b, s_new = _hgrn2_mixer(qb, fb, ib, gb, lb[l], s0, hgrn_norm_g[l])
            x2 = _merge_call(x2, to_t(o_a), to_t(o_b), p, mods, wpa_b, wpb_b, wo_b, l=l, tm=tm, seq_tiles=seq_tiles)
            x2 = _ffn_call(x2, norm_g, mods, ffn_w_in_b, ffn_w_out_b, final_g, l=l, s=1, sub=2, tm=tm, tf=256,
                           seq_tiles=seq_tiles, final=(l == depth - 1))
            kv_rows.append(kv_new[:, :, :4])
            win_bufs.append(new_win)
            hg_states.append(s_new)
        return to_bl(x2), jnp.stack(kv_rows), jnp.stack(win_bufs), jnp.stack(hg_states)

    def prompt_past(l):
        return None, None, jnp.zeros((bp, HG_HEADS, HG_DK, HG_DV), F32)

    def sample_past(l):
        past = cache_nsa_kv[l][page_table].reshape(n_seq, n_pages * page, 4, NSA_GROUPS, NSA_DH)
        return past, state_win_kv[l], state_hgrn[l]

    tm_p = 1024 if lp % 1024 == 0 else lp
    y_p, kv_p, win_p, hg_p = trunk(x_prompt.reshape(bp * lp, d), nb=bp, ln=lp, tm=tm_p,
                                   seq_tiles=(bs, lp // tm_p), token_major=False, past_fn=prompt_past)
    xs_tm = x_sample.transpose(1, 0, 2).reshape(ls * bs, d)
    y_s, kv_s, win_s, hg_s = trunk(xs_tm, nb=bs, ln=ls, tm=ls * bs, seq_tiles=None, token_major=True,
                                   past_fn=sample_past)
    return (y_p, y_s, kv_p, kv_s, win_p, win_s, hg_p, hg_s)
```

```python
import functools
import math

import jax
import jax.numpy as jnp
import numpy as np
from jax import lax
from jax.experimental import pallas as pl
from jax.experimental.pallas import tpu as pltpu

NSA_HEADS = 8
NSA_GROUPS = 2
NSA_HPG = NSA_HEADS // NSA_GROUPS
NSA_DH = 64
NSA_WIDTH = NSA_HEADS * NSA_DH
NSA_SCALE = NSA_DH ** -0.5
CMP_LEN = 32
CMP_STRIDE = 16
CMP_RATIO = CMP_LEN // CMP_STRIDE
CMP_HID = 128
SEL_LEN = 64
SEL_TOP = 16
WINDOW = 512
SEL_QBLOCK = 64
WIN_QBLOCK = 128
HG_HEADS = 4
HG_DK = 128
HG_DV = 128
HG_WIDTH = HG_HEADS * HG_DV
HG_CHUNK = 64
REL_BUCKETS = 32
REL_MAX_DIST = 128
N_ADA = 9
EPS = 1e-6

BF16 = jnp.bfloat16
F32 = jnp.float32

VMEM_LIMIT_BYTES = 56 * 1024 * 1024

P_MA, P_MB, P_QA, P_KV4, P_WING, P_QB, P_FB, P_IB, P_GB, P_TOTAL = (
    0, 1024, 2048, 2560, 3072, 3584, 4096, 4608, 5120, 5632)


def _proj_column_index(d_model):
    sizes = (NSA_WIDTH, 6 * NSA_GROUPS * NSA_DH, 3 * NSA_HEADS, HG_HEADS * HG_DK, HG_HEADS * HG_DK,
             HG_WIDTH, HG_WIDTH, d_model, d_model)
    offs = np.concatenate([[0], np.cumsum(sizes)])
    d_in = int(offs[-1])
    qa, kva, ga, qb, fb, ib, gb, ma, mb = [np.arange(offs[i], offs[i + 1]) for i in range(9)]
    pad = np.full((512 - 256 - ga.size,), d_in)
    idx = np.concatenate([ma, mb, qa, kva[:512], kva[512:], ga, pad, qb, fb, ib, gb])
    assert idx.size == P_TOTAL and d_model == 1024
    return idx.astype(np.int32), d_in


def _mods_kernel(c_ref, w_ref, b_ref, o_ref):
    c = c_ref[...]
    sc = (c * jax.nn.sigmoid(c)).astype(BF16)
    o_ref[...] = jnp.dot(sc, w_ref[...].astype(BF16), preferred_element_type=F32) + b_ref[...]


def _mods_call(c_all, ada_w, ada_b):
    depth, d, nd = ada_w.shape
    n = c_all.shape[0]
    b3 = ada_b.reshape(depth, 1, nd)
    return pl.pallas_call(
        _mods_kernel,
        grid=(depth, nd // d),
        in_specs=[
            pl.BlockSpec((n, d), lambda l, j: (0, 0)),
            pl.BlockSpec((None, d, d), lambda l, j: (l, 0, j)),
            pl.BlockSpec((None, 1, d), lambda l, j: (l, 0, j)),
        ],
        out_specs=pl.BlockSpec((None, None, n, d), lambda l, j: (l, j, 0, 0)),
        out_shape=jax.ShapeDtypeStruct((depth, nd // d, n, d), F32),
        compiler_params=pltpu.CompilerParams(dimension_semantics=("arbitrary", "arbitrary")),
        name="adaln_mods",
    )(c_all, ada_w, b3)


def _rows(m, tm):
    r = m.shape[0]
    if r == 1 or r == tm:
        return m
    return jnp.concatenate([m] * (tm // r), axis=0)


def _norm_mod(x, g, shift, scale):
    y = x * lax.rsqrt(jnp.mean(x * x, axis=-1, keepdims=True) + EPS) * g
    return y * (1.0 + scale) + shift


def _mod_spec(mods, l, k, tm, seq_tiles):
    if seq_tiles is not None:
        row0, tiles_per_seq = seq_tiles
        d = mods.shape[-1]
        m5 = mods.reshape(mods.shape[0], mods.shape[1], mods.shape[2], 1, d)
        return m5, pl.BlockSpec((None, None, None, 1, d),
                                lambda i, j, l=l, k=k: (l, k, row0 + i // tiles_per_seq, 0, 0))
    d = mods.shape[-1]
    return mods, pl.BlockSpec((None, None, 128, d), lambda i, j, l=l, k=k: (l, k, 0, 0))


def _ffn_kernel(x_ref, g_ref, sh_ref, sc_ref, gt_ref, wa_ref, wb_ref, wo_ref, fg_ref, o_ref,
                n_scr, acc_scr, *, nf, final):
    j = pl.program_id(1)
    tm = x_ref.shape[0]

    @pl.when(j == 0)
    def _():
        n = _norm_mod(x_ref[...], g_ref[...], _rows(sh_ref[...], tm), _rows(sc_ref[...], tm))
        n_scr[...] = n.astype(BF16)
        acc_scr[...] = jnp.zeros_like(acc_scr)

    n = n_scr[...]
    a = jnp.dot(n, wa_ref[...], preferred_element_type=F32)
    b = jnp.dot(n, wb_ref[...], preferred_element_type=F32)
    h = (a * jax.nn.sigmoid(a) * b).astype(BF16)
    acc_scr[...] += jnp.dot(h, wo_ref[...], preferred_element_type=F32)

    @pl.when(j == nf - 1)
    def _():
        y = x_ref[...] + 0.5 * _rows(gt_ref[...], tm) * acc_scr[...]
        if final:
            y = y * lax.rsqrt(jnp.mean(y * y, axis=-1, keepdims=True) + EPS) * fg_ref[...]
        o_ref[...] = y


def _ffn_call(x, norm_g, mods, ffn_w_in, ffn_w_out, final_g, *, l, s, sub, tm, tf, seq_tiles, final):
    t, d = x.shape
    d_ff = ffn_w_out.shape[2]
    nf = d_ff // tf
    g2 = norm_g[l, sub].reshape(1, d)
    fg2 = final_g.reshape(1, d)
    m_sh, sp_sh = _mod_spec(mods, l, 3 * sub, tm, seq_tiles)
    m_sc, sp_sc = _mod_spec(mods, l, 3 * sub + 1, tm, seq_tiles)
    m_gt, sp_gt = _mod_spec(mods, l, 3 * sub + 2, tm, seq_tiles)
    return pl.pallas_call(
        functools.partial(_ffn_kernel, nf=nf, final=final),
        grid=(t // tm, nf),
        in_specs=[
            pl.BlockSpec((tm, d), lambda i, j: (i, 0)),
            pl.BlockSpec((1, d), lambda i, j: (0, 0)),
            sp_sh, sp_sc, sp_gt,
            pl.BlockSpec((None, None, d, tf), lambda i, j: (l, s, 0, j)),
            pl.BlockSpec((None, None, d, tf), lambda i, j: (l, s, 0, nf + j)),
            pl.BlockSpec((None, None, tf, d), lambda i, j: (l, s, j, 0)),
            pl.BlockSpec((1, d), lambda i, j: (0, 0)),
        ],
        out_specs=pl.BlockSpec((tm, d), lambda i, j: (i, 0)),
        out_shape=jax.ShapeDtypeStruct((t, d), F32),
        scratch_shapes=[pltpu.VMEM((tm, d), BF16), pltpu.VMEM((tm, d), F32)],
        compiler_params=pltpu.CompilerParams(
            dimension_semantics=("parallel", "arbitrary"), vmem_limit_bytes=VMEM_LIMIT_BYTES),
        name="ffn",
    )(x, g2, m_sh, m_sc, m_gt, ffn_w_in, ffn_w_in, ffn_w_out, fg2)


def _proj_kernel(x_ref, g_ref, sh_ref, sc_ref, w_ref, o_ref, n_scr):
    j = pl.program_id(1)
    tm = x_ref.shape[0]

    @pl.when(j == 0)
    def _():
        n = _norm_mod(x_ref[...], g_ref[...], _rows(sh_ref[...], tm), _rows(sc_ref[...], tm))
        n_scr[...] = n.astype(BF16)

    o_ref[...] = jnp.dot(n_scr[...], w_ref[...], preferred_element_type=F32)


def _proj_call(x, norm_g, mods, w_in_p, *, l, tm, tn, seq_tiles):
    t, d = x.shape
    n_out = w_in_p.shape[2]
    g2 = norm_g[l, 1].reshape(1, d)
    m_sh, sp_sh = _mod_spec(mods, l, 3, tm, seq_tiles)
    m_sc, sp_sc = _mod_spec(mods, l, 4, tm, seq_tiles)
    return pl.pallas_call(
        _proj_kernel,
        grid=(t // tm, n_out // tn),
        in_specs=[
            pl.BlockSpec((tm, d), lambda i, j: (i, 0)),
            pl.BlockSpec((1, d), lambda i, j: (0, 0)),
            sp_sh, sp_sc,
            pl.BlockSpec((None, d, tn), lambda i, j: (l, 0, j)),
        ],
        out_specs=pl.BlockSpec((tm, tn), lambda i, j: (i, j)),
        out_shape=jax.ShapeDtypeStruct((t, n_out), F32),
        scratch_shapes=[pltpu.VMEM((tm, d), BF16)],
        compiler_params=pltpu.CompilerParams(
            dimension_semantics=("parallel", "arbitrary"), vmem_limit_bytes=VMEM_LIMIT_BYTES),
        name="in_proj",
    )(x, g2, m_sh, m_sc, w_in_p)


def _merge_kernel(x_ref, oa_ref, ob_ref, ma_ref, mb_ref, gt_ref, wpa_ref, wpb_ref, wo_ref, o_ref):
    tm = x_ref.shape[0]
    ya = jnp.dot(oa_ref[...].astype(BF16), wpa_ref[...], preferred_element_type=F32)
    yb = jnp.dot(ob_ref[...].astype(BF16), wpb_ref[...], preferred_element_type=F32)
    merged = jax.nn.sigmoid(ma_ref[...]) * ya + jax.nn.sigmoid(mb_ref[...]) * yb
    y = jnp.dot(merged.astype(BF16), wo_ref[...], preferred_element_type=F32)
    o_ref[...] = x_ref[...] + _rows(gt_ref[...], tm) * y


def _merge_call(x, o_a, o_b, p, mods, wpa, wpb, wo, *, l, tm, seq_tiles):
    t, d = x.shape
    m_gt, sp_gt = _mod_spec(mods, l, 5, tm, seq_tiles)
    one = lambda i, j: (i, 0)
    return pl.pallas_call(
        _merge_kernel,
        grid=(t // tm, 1),
        in_specs=[
            pl.BlockSpec((tm, d), one),
            pl.BlockSpec((tm, NSA_WIDTH), one),
            pl.BlockSpec((tm, HG_WIDTH), one),
            pl.BlockSpec((tm, d), lambda i, j: (i, P_MA // d)),
            pl.BlockSpec((tm, d), lambda i, j: (i, P_MB // d)),
            sp_gt,
            pl.BlockSpec((None, NSA_WIDTH, d), lambda i, j: (l, 0, 0)),
            pl.BlockSpec((None, HG_WIDTH, d), lambda i, j: (l, 0, 0)),
            pl.BlockSpec((None, d, d), lambda i, j: (l, 0, 0)),
        ],
        out_specs=pl.BlockSpec((tm, d), one),
        out_shape=jax.ShapeDtypeStruct((t, d), F32),
        compiler_params=pltpu.CompilerParams(
            dimension_semantics=("parallel", "arbitrary"), vmem_limit_bytes=VMEM_LIMIT_BYTES),
        name="merge_out",
    )(x, o_a, o_b, p, p, m_gt, wpa, wpb, wo)


NEG = -1e30


def _bucket_thresholds():
    n = np.arange(0, 4 * REL_MAX_DIST)
    exact = REL_BUCKETS // 2
    large = exact + np.floor(np.log(np.maximum(n, 1) / exact) / math.log(REL_MAX_DIST / exact)
                             * (REL_BUCKETS - exact)).astype(np.int64)
    bucket = np.where(n < exact, n, np.minimum(large, REL_BUCKETS - 1))
    return [int(np.min(n[bucket >= k])) for k in range(REL_BUCKETS)]


_THRESHOLDS = _bucket_thresholds()


def _bias_kernel(tab_ref, a_ref, b_ref, o_ref, *, hi):
    dist = a_ref[...] - b_ref[...]
    tab = tab_ref[...]
    v = jnp.broadcast_to(tab[:, REL_BUCKETS - 1:REL_BUCKETS], dist.shape)
    for k in range(REL_BUCKETS - 2, -1, -1):
        v = jnp.where(dist < _THRESHOLDS[k + 1], tab[:, k:k + 1], v)
    o_ref[...] = jnp.where((dist >= 0) & (dist < hi), v, NEG)


def _bias_table(rel_table, a, b, head, hi):
    r, c = a.shape[0], b.shape[0]
    tr = min(r, 512)
    assert r % tr == 0
    tab = jnp.take(rel_table, jnp.asarray(head, jnp.int32), axis=1).T
    return pl.pallas_call(
        functools.partial(_bias_kernel, hi=hi),
        grid=(r // tr,),
        in_specs=[pl.BlockSpec((tr, REL_BUCKETS), lambda i: (i, 0)),
                  pl.BlockSpec((tr, 1), lambda i: (i, 0)),
                  pl.BlockSpec((1, c), lambda i: (0, 0))],
        out_specs=pl.BlockSpec((tr, c), lambda i: (i, 0)),
        out_shape=jax.ShapeDtypeStruct((r, c), F32),
        name="rel_bias_table",
    )(tab, jnp.asarray(a, jnp.int32).reshape(r, 1), jnp.asarray(b, jnp.int32).reshape(1, c))


BIG = 1 << 30


def _dot_nt(a, b):
    return lax.dot_general(a, b, (((1,), (1,)), ((), ())), preferred_element_type=F32)


def _dot(a, b):
    return jnp.dot(a, b, preferred_element_type=F32)


def _silu(x):
    return x * jax.nn.sigmoid(x)


def _compress_weights(cmp_w1, cmp_b1, cmp_w2, cmp_b2):
    eye = jnp.eye(NSA_GROUPS, dtype=F32)
    w1 = jnp.einsum('lkrjde,gh->lkjgdrhe', cmp_w1, eye)
    depth = cmp_w1.shape[0]
    w1 = w1.reshape(depth, 2, CMP_STRIDE * NSA_GROUPS * NSA_DH, CMP_RATIO * NSA_GROUPS * CMP_HID).astype(BF16)
    w2 = jnp.einsum('lked,gh->lkgehd', cmp_w2, eye).reshape(depth, 2, NSA_GROUPS * CMP_HID, NSA_GROUPS * NSA_DH)
    b1 = jnp.tile(cmp_b1, (1, 1, NSA_GROUPS)).reshape(depth, 2, 1, NSA_GROUPS * CMP_HID)
    b2 = jnp.tile(cmp_b2, (1, 1, NSA_GROUPS)).reshape(depth, 2, 1, NSA_GROUPS * NSA_DH)
    return w1, b1, w2.astype(BF16), b2


def _compress_rows(load_piece, n_sub, w1, b1, w2, b2):
    x = jnp.concatenate([load_piece(j).astype(BF16) for j in range(CMP_STRIDE)], axis=1)
    hr = _dot(x, w1)
    half = NSA_GROUPS * CMP_HID
    h = hr[:, :half] + pltpu.roll(hr[:, half:], n_sub - 1, 0) + b1
    return _dot(_silu(h).astype(BF16), w2) + b2


def _topk_member_rows(sc, n_sel, n_top):
    blk = lax.broadcasted_iota(jnp.int32, sc.shape, 0)
    cnt = jnp.zeros(sc.shape, jnp.int32)
    for i in range(n_sel):
        si = sc[i:i + 1, :]
        ahead = (si > sc) | ((si == sc) & (blk > i))
        cnt = cnt + ahead.astype(jnp.int32)
    return ((cnt < n_top) & (blk < n_sel)).astype(F32)


def _topk_member_lanes(sc, n_sel, n_top):
    blk = lax.broadcasted_iota(jnp.int32, sc.shape, 1)
    cnt = jnp.zeros(sc.shape, jnp.int32)
    for i in range(n_sel):
        si = sc[:, i:i + 1]
        ahead = (si > sc) | ((si == sc) & (blk > i))
        cnt = cnt + ahead.astype(jnp.int32)
    return ((cnt < n_top) & (blk < n_sel)).astype(F32)


def _softmax_rows(lg):
    m = jnp.max(lg, axis=-1, keepdims=True)
    e = jnp.exp(lg - m)
    s = jnp.sum(e, axis=-1, keepdims=True)
    return jnp.where(m > 0.5 * NEG, e / s, 0.0)


def _decode_kernel(pt_ref, cache_ref, q_ref, kvn_ref, winn_ref, winbuf_ref, gate_ref,
                   w1_ref, b1_ref, w2_ref, b2_ref, bcmp_ref, bsel_ref, bwin_ref, ovl_ref, esel_ref,
                   o_ref, newwin_ref, buf, wbuf, cbuf, sem, *, l, n_pages, page, ls, past, n_sel, n_top):
    s = pl.program_id(0)
    n_seq = pl.num_programs(0)
    slot = s % 2
    t_rows = buf.shape[1]
    w_rows = wbuf.shape[0]
    w_len = winbuf_ref.shape[0]

    def page_copy(seq, sl, pg):
        return pltpu.make_async_copy(cache_ref.at[l, pt_ref[seq, pg]],
                                     buf.at[sl, pl.ds(pg * page, page)], sem.at[sl])

    def start_all(seq, sl):
        for pg in range(n_pages):
            page_copy(seq, sl, pg).start()

    @pl.when(s == 0)
    def _():
        start_all(0, 0)
        for sl in range(2):
            buf[sl, pl.ds(past, t_rows - past), :] = jnp.zeros((t_rows - past, buf.shape[2]), F32)
        wbuf[pl.ds(w_len, w_rows - w_len), :] = jnp.zeros((w_rows - w_len, wbuf.shape[1]), F32)

    @pl.when(s + 1 < n_seq)
    def _():
        start_all(s + 1, 1 - slot)

    for pg in range(n_pages):
        page_copy(s, slot, pg).wait()

    xb = buf.at[slot]
    xb[pl.ds(past, ls), :] = kvn_ref[...]
    wbuf[pl.ds(0, w_len), :] = winbuf_ref[...]
    wbuf[pl.ds(w_len, ls), :] = winn_ref[...]
    newwin_ref[pl.ds(0, w_len - ls), :] = winbuf_ref[pl.ds(ls, w_len - ls), :]
    newwin_ref[pl.ds(w_len - ls, ls), :] = winn_ref[...]

    n_sub = past // CMP_STRIDE
    gw = NSA_GROUPS * NSA_DH
    summ = []
    for k in range(2):
        cbuf[k] = xb[pl.ds(0, past), k * gw:(k + 1) * gw]
        summ.append(_compress_rows(
            lambda j, k=k: cbuf[k, pl.ds(j, n_sub, stride=CMP_STRIDE), :],
            n_sub, w1_ref[k], b1_ref[k], w2_ref[k], b2_ref[k]))
    ck, cv = summ

    rows = NSA_HPG * ls
    tokpos = past + lax.broadcasted_iota(jnp.int32, (ls, 128), 0)
    blk = lax.broadcasted_iota(jnp.int32, (ls, 128), 1)
    cur = tokpos // SEL_LEN
    forced = (blk == 0) | (blk == cur) | (blk == cur - 1)
    for g in range(NSA_GROUPS):
        lo, hi = g * NSA_DH, (g + 1) * NSA_DH
        qg = (q_ref[g] * NSA_SCALE).astype(BF16)
        gt = jax.nn.sigmoid(gate_ref[g])
        p = _softmax_rows(_dot_nt(qg, ck[:, lo:hi].astype(BF16)) + bcmp_ref[g])
        pb = p.astype(BF16)
        o_cmp = _dot(pb, cv[:, lo:hi].astype(BF16))
        ps = _dot(pb, ovl_ref[...])
        score = ps[0:ls]
        for r in range(1, NSA_HPG):
            score = score + ps[r * ls:(r + 1) * ls]
        sc = jnp.where(forced, jnp.inf, jnp.where(blk <= cur, score, -jnp.inf))
        member = _topk_member_lanes(sc, n_sel, n_top)
        mem = jnp.concatenate([member] * NSA_HPG, axis=0).astype(BF16)
        addmask = (_dot(mem, esel_ref[...]) - 1.0) * (-NEG)
        ksel = xb[:, 2 * gw + lo:2 * gw + hi].astype(BF16)
        vsel = xb[:, 3 * gw + lo:3 * gw + hi].astype(BF16)
        p = _softmax_rows(_dot_nt(qg, ksel) + bsel_ref[g] + addmask)
        o_sel = _dot(p.astype(BF16), vsel)
        kwin = wbuf[:, lo:hi].astype(BF16)
        vwin = wbuf[:, gw + lo:gw + hi].astype(BF16)
        p = _softmax_rows(_dot_nt(qg, kwin) + bwin_ref[g])
        o_win = _dot(p.astype(BF16), vwin)
        o_ref[g] = gt[:, 0:1] * o_cmp + gt[:, 1:2] * o_sel + gt[:, 2:3] * o_win


def _decode_call(cache4, page_table, q_s, kvn, winn, winbuf, gates, cw, tabs, *, l, past, n_sel):
    n_seq, n_pages = page_table.shape
    page = cache4.shape[2]
    ls = kvn.shape[1]
    rows = NSA_HPG * ls
    w_len = winbuf.shape[1]
    t_rows = tabs["sel"].shape[-1]
    w_rows = tabs["win"].shape[-1]
    w1, b1, w2, b2 = cw
    n_top = min(SEL_TOP, n_sel)
    kern = functools.partial(_decode_kernel, l=l, n_pages=n_pages, page=page, ls=ls, past=past,
                             n_sel=n_sel, n_top=n_top)
    whole = lambda shape: pl.BlockSpec(shape, lambda s, pt: (0,) * len(shape))
    grid_spec = pltpu.PrefetchScalarGridSpec(
        num_scalar_prefetch=1,
        grid=(n_seq,),
        in_specs=[
            pl.BlockSpec(memory_space=pl.ANY),
            pl.BlockSpec((None, NSA_GROUPS, rows, NSA_DH), lambda s, pt: (s, 0, 0, 0)),
            pl.BlockSpec((None, ls, 4 * NSA_GROUPS * NSA_DH), lambda s, pt: (s, 0, 0)),
            pl.BlockSpec((None, ls, 2 * NSA_GROUPS * NSA_DH), lambda s, pt: (s, 0, 0)),
            pl.BlockSpec((None, w_len, 2 * NSA_GROUPS * NSA_DH), lambda s, pt: (s, 0, 0)),
            pl.BlockSpec((None, NSA_GROUPS, rows, 3), lambda s, pt: (s, 0, 0, 0)),
            pl.BlockSpec((None,) + w1.shape[1:], lambda s, pt: (l, 0, 0, 0)),
            pl.BlockSpec((None,) + b1.shape[1:], lambda s, pt: (l, 0, 0, 0)),
            pl.BlockSpec((None,) + w2.shape[1:], lambda s, pt: (l, 0, 0, 0)),
            pl.BlockSpec((None,) + b2.shape[1:], lambda s, pt: (l, 0, 0, 0)),
            whole(tabs["cmp"].shape), whole(tabs["sel"].shape), whole(tabs["win"].shape),
            whole(tabs["ovl"].shape), whole(tabs["esel"].shape),
        ],
        out_specs=[
            pl.BlockSpec((None, NSA_GROUPS, rows, NSA_DH), lambda s, pt: (s, 0, 0, 0)),
            pl.BlockSpec((None, w_len, 2 * NSA_GROUPS * NSA_DH), lambda s, pt: (s, 0, 0)),
        ],
        scratch_shapes=[pltpu.VMEM((2, t_rows, 4 * NSA_GROUPS * NSA_DH), F32),
                        pltpu.VMEM((w_rows, 2 * NSA_GROUPS * NSA_DH), F32),
                        pltpu.VMEM((2, past, NSA_GROUPS * NSA_DH), F32),
                        pltpu.SemaphoreType.DMA((2,))],
    )
    return pl.pallas_call(
        kern,
        grid_spec=grid_spec,
        out_shape=[jax.ShapeDtypeStruct((n_seq, NSA_GROUPS, rows, NSA_DH), F32),
                   jax.ShapeDtypeStruct((n_seq, w_len, 2 * NSA_GROUPS * NSA_DH), F32)],
        compiler_params=pltpu.CompilerParams(dimension_semantics=("arbitrary",),
                                             vmem_limit_bytes=VMEM_LIMIT_BYTES),
        name="nsa_decode",
    )(page_table, cache4, q_s, kvn, winn, winbuf, gates, w1, b1, w2, b2,
      tabs["cmp"], tabs["sel"], tabs["win"], tabs["ovl"], tabs["esel"])


def _decode_tables(rel_table, *, past, ls, n_cmp, n_sel, w_len):
    rows = NSA_HPG * ls
    t_rows = -(-(n_sel * SEL_LEN) // 128) * 128
    w_rows = -(-(w_len + ls) // 128) * 128
    n_sub = past // CMP_STRIDE
    tok = np.tile(np.arange(ls), NSA_HPG)
    hd = np.repeat(np.arange(NSA_HPG), ls)
    a = np.concatenate([past + tok, past + tok])
    head = np.concatenate([hd, NSA_HPG + hd])
    bc = np.where(np.arange(n_sub) < n_cmp, np.arange(n_sub) * CMP_STRIDE + CMP_LEN - 1, BIG)
    tabs = {
        "cmp": _bias_table(rel_table, a, bc, head, BIG).reshape(NSA_GROUPS, rows, n_sub),
        "sel": _bias_table(rel_table, a, np.arange(t_rows), head, BIG).reshape(NSA_GROUPS, rows, t_rows),
        "win": _bias_table(rel_table, a, past - w_len + np.arange(w_rows), head, WINDOW
                           ).reshape(NSA_GROUPS, rows, w_rows),
    }
    ovl = np.zeros((n_sub, 128), np.float32)
    ovl[:n_cmp, :n_sel] = _block_overlap(n_cmp, n_sel)
    esel = np.zeros((128, t_rows), np.float32)
    esel[np.arange(n_sel * SEL_LEN) // SEL_LEN, np.arange(n_sel * SEL_LEN)] = 1.0
    tabs["ovl"] = jnp.asarray(ovl, BF16)
    tabs["esel"] = jnp.asarray(esel, BF16)
    return tabs


TQ = 128
TKS = 256
TKW = 128
SEL_NEAR = 5


def _prompt_tables(rel_table, *, ln):
    nq = ln // TQ
    n_sub = ln // CMP_STRIDE
    n_cmp = (ln - CMP_LEN) // CMP_STRIDE + 1
    n_sel = -(-ln // SEL_LEN)
    qq = np.tile(np.arange(TQ), NSA_HPG)
    hd = np.repeat(np.arange(NSA_HPG), TQ)
    rows = NSA_HPG * TQ
    a = np.concatenate([TQ * i + qq - (CMP_LEN - 1) for i in range(nq) for g in range(NSA_GROUPS)])
    head = np.concatenate([NSA_HPG * g + hd for i in range(nq) for g in range(NSA_GROUPS)])
    bc = np.where(np.arange(n_sub) < n_cmp, np.arange(n_sub) * CMP_STRIDE, BIG)
    cmp_t = _bias_table(rel_table, a, bc, head, BIG).reshape(nq, NSA_GROUPS, rows, n_sub)
    a = np.concatenate([SEL_NEAR * TQ - TQ * u + qq for g in range(NSA_GROUPS) for u in range(SEL_NEAR + 1)])
    head = np.concatenate([NSA_HPG * g + hd for g in range(NSA_GROUPS) for u in range(SEL_NEAR + 1)])
    sel_t = _bias_table(rel_table, a, np.arange(TKS), head, BIG).reshape(NSA_GROUPS, SEL_NEAR + 1, rows, TKS)
    nwt = WINDOW // TKW + 1
    a = np.concatenate([WINDOW - TKW * j + qq for g in range(NSA_GROUPS) for j in range(nwt)])
    head = np.concatenate([NSA_HPG * g + hd for g in range(NSA_GROUPS) for j in range(nwt)])
    win_t = _bias_table(rel_table, a, np.arange(TKW), head, WINDOW).reshape(NSA_GROUPS, nwt, rows, TKW)
    nb = -(-n_sel // 8) * 8
    ovl_t = np.zeros((nb, n_sub), np.float32)
    ovl_t[:n_sel, :n_cmp] = _block_overlap(n_cmp, n_sel).T
    esel = np.zeros((128, ln), np.float32)
    esel[np.arange(ln) // SEL_LEN, np.arange(ln)] = 1.0
    return {"cmp": cmp_t, "sel": sel_t, "win": win_t, "ovl_t": jnp.asarray(ovl_t, BF16),
            "esel": jnp.asarray(esel, BF16), "n_sel": n_sel}


def _pcompress_kernel(k_ref, v_ref, w1_ref, b1_ref, w2_ref, b2_ref, ck_ref, cv_ref, *, n_sub):
    for k, src, out in ((0, k_ref, ck_ref), (1, v_ref, cv_ref)):
        out[...] = _compress_rows(
            lambda j, src=src: src[pl.ds(j, n_sub, stride=CMP_STRIDE), :],
            n_sub, w1_ref[k], b1_ref[k], w2_ref[k], b2_ref[k])


def _pcompress_call(p, cw, *, l, nb, ln):
    w1, b1, w2, b2 = cw
    n_sub = ln // CMP_STRIDE
    gw = NSA_GROUPS * NSA_DH
    lay = lambda a: pl.BlockSpec((None,) + a.shape[1:], lambda b: (l, 0, 0, 0))
    out = pl.BlockSpec((None, n_sub, gw), lambda b: (b, 0, 0))
    return pl.pallas_call(
        functools.partial(_pcompress_kernel, n_sub=n_sub),
        grid=(nb,),
        in_specs=[pl.BlockSpec((ln, gw), lambda b: (b, P_KV4 // gw)),
                  pl.BlockSpec((ln, gw), lambda b: (b, P_KV4 // gw + 1)),
                  lay(w1), lay(b1), lay(w2), lay(b2)],
        out_specs=[out, out],
        out_shape=[jax.ShapeDtypeStruct((nb, n_sub, gw), F32)] * 2,
        compiler_params=pltpu.CompilerParams(dimension_semantics=("parallel",),
                                             vmem_limit_bytes=VMEM_LIMIT_BYTES),
        name="nsa_compress",
    )(p, p, w1, b1, w2, b2)


def _group_queries(q_tile, g):
    hs = [q_tile[:, (NSA_HPG * g + r) * NSA_DH:(NSA_HPG * g + r + 1) * NSA_DH] for r in range(NSA_HPG)]
    return (jnp.concatenate(hs, axis=0) * NSA_SCALE).astype(BF16)


def _pcmp_kernel(q_ref, ck_ref, cv_ref, bias_ref, wing_ref, ovl_ref, o_ref, mem_ref, *, n_sel, n_top):
    i = pl.program_id(1)
    gw = NSA_GROUPS * NSA_DH
    gates = jax.nn.sigmoid(wing_ref[:, 2 * gw:2 * gw + 3 * NSA_HEADS])
    nbk = ovl_ref.shape[0]
    blk = lax.broadcasted_iota(jnp.int32, (nbk, TQ), 0)
    cur = (i * TQ + lax.broadcasted_iota(jnp.int32, (nbk, TQ), 1)) // SEL_LEN
    forced = (blk == 0) | (blk == cur) | (blk == cur - 1)
    q_tile = q_ref[...]
    outs = []
    for g in range(NSA_GROUPS):
        lo, hi = g * NSA_DH, (g + 1) * NSA_DH
        qg = _group_queries(q_tile, g)
        p = _softmax_rows(_dot_nt(qg, ck_ref[:, lo:hi].astype(BF16)) + bias_ref[g])
        pb = p.astype(BF16)
        o = _dot(pb, cv_ref[:, lo:hi].astype(BF16))
        score_t = _dot_nt(ovl_ref[...], pb[0:TQ])
        for r in range(1, NSA_HPG):
            score_t = score_t + _dot_nt(ovl_ref[...], pb[r * TQ:(r + 1) * TQ])
        sc = jnp.where(forced, jnp.inf, jnp.where(blk <= cur, score_t, -jnp.inf))
        member_t = _topk_member_rows(sc, n_sel, n_top)
        member_t = jnp.concatenate([member_t, jnp.zeros((128 - nbk, TQ), F32)], axis=0)
        mem_ref[g] = member_t.T.astype(BF16)
        for r in range(NSA_HPG):
            h = NSA_HPG * g + r
            outs.append(gates[:, h:h + 1] * o[r * TQ:(r + 1) * TQ])
    o_ref[...] = jnp.concatenate(outs, axis=1)


def _pcmp_call(p, ck, cv, tabs, *, nb, ln):
    nq = ln // TQ
    n_sub = ln // CMP_STRIDE
    gw = NSA_GROUPS * NSA_DH
    rows = NSA_HPG * TQ
    n_sel = tabs["n_sel"]
    cblk = pl.BlockSpec((None, n_sub, gw), lambda b, i: (b, 0, 0))
    return pl.pallas_call(
        functools.partial(_pcmp_kernel, n_sel=n_sel, n_top=min(SEL_TOP, n_sel)),
        grid=(nb, nq),
        in_specs=[pl.BlockSpec((TQ, NSA_WIDTH), lambda b, i: (b * nq + i, P_QA // NSA_WIDTH)),
                  cblk, cblk,
                  pl.BlockSpec((None, NSA_GROUPS, rows, n_sub), lambda b, i: (i, 0, 0, 0)),
                  pl.BlockSpec((TQ, 4 * gw), lambda b, i: (b * nq + i, P_WING // (4 * gw))),
                  pl.BlockSpec(tabs["ovl_t"].shape, lambda b, i: (0, 0))],
        out_specs=[pl.BlockSpec((TQ, NSA_WIDTH), lambda b, i: (b * nq + i, 0)),
                   pl.BlockSpec((None, NSA_GROUPS, TQ, 128), lambda b, i: (b, 0, i, 0))],
        out_shape=[jax.ShapeDtypeStruct((nb * ln, NSA_WIDTH), F32),
                   jax.ShapeDtypeStruct((nb, NSA_GROUPS, ln, 128), BF16)],
        compiler_params=pltpu.CompilerParams(dimension_semantics=("parallel", "arbitrary"),
                                             vmem_limit_bytes=VMEM_LIMIT_BYTES),
        name="nsa_cmp_select",
    )(p, ck, cv, tabs["cmp"], p, tabs["ovl_t"])


def _flash_step(carry, lg, v_tile):
    m, l, acc = carry
    m_new = jnp.maximum(m, jnp.max(lg, axis=-1, keepdims=True))
    alpha = jnp.exp(m - m_new)
    e = jnp.exp(lg - m_new)
    l = alpha * l + jnp.sum(e, axis=-1, keepdims=True)
    acc = alpha * acc + _dot(e.astype(BF16), v_tile)
    return m_new, l, acc


def _pattn_kernel(q_ref, kv_ref, wkv_ref, wing_ref, mem_ref, ocmp_ref, bsel_ref, bwin_ref, esel_ref, o_ref,
                  ks_scr, vs_scr, kw_scr, vw_scr, madd_scr):
    i = pl.program_id(1)
    gw = NSA_GROUPS * NSA_DH
    rows = NSA_HPG * TQ

    @pl.when(i == 0)
    def _():
        for g in range(NSA_GROUPS):
            lo, hi = g * NSA_DH, (g + 1) * NSA_DH
            ks_scr[g] = kv_ref[:, 2 * gw + lo:2 * gw + hi].astype(BF16)
            vs_scr[g] = kv_ref[:, 3 * gw + lo:3 * gw + hi].astype(BF16)
            kw_scr[g] = wkv_ref[:, lo:hi].astype(BF16)
            vw_scr[g] = wkv_ref[:, gw + lo:gw + hi].astype(BF16)

    gates = jax.nn.sigmoid(wing_ref[:, 2 * gw:2 * gw + 3 * NSA_HEADS])
    q_tile = q_ref[...]
    init = (jnp.full((rows, 1), NEG, F32), jnp.zeros((rows, 1), F32), jnp.zeros((rows, NSA_DH), F32))
    n_kt = (i * TQ + TQ - 1) // TKS + 1
    w_lo = jnp.maximum(i - WINDOW // TKW, 0)
    outs = []
    for g in range(NSA_GROUPS):
        qg = _group_queries(q_tile, g)
        madd_scr[g] = (_dot(mem_ref[g], esel_ref[...]) - 1.0) * (-NEG)

        def sel_body(kt, carry, g=g, qg=qg):
            k0 = pl.multiple_of(kt * TKS, TKS)
            u = jnp.maximum(SEL_NEAR - (i - kt * (TKS // TQ)), 0)
            lg = _dot_nt(qg, ks_scr[g, pl.ds(k0, TKS), :]) + bsel_ref[g, u]
            lg = (lg.reshape(NSA_HPG, TQ, TKS) + madd_scr[g, :, pl.ds(k0, TKS)][None]).reshape(rows, TKS)
            return _flash_step(carry, lg, vs_scr[g, pl.ds(k0, TKS), :])

        _, l_s, acc_s = lax.fori_loop(0, n_kt, sel_body, init)

        def win_body(kt, carry, g=g, qg=qg):
            k0 = pl.multiple_of(kt * TKW, TKW)
            j = kt - (i - WINDOW // TKW)
            lg = _dot_nt(qg, kw_scr[g, pl.ds(k0, TKW), :]) + bwin_ref[g, j]
            return _flash_step(carry, lg, vw_scr[g, pl.ds(k0, TKW), :])

        _, l_w, acc_w = lax.fori_loop(w_lo, i + 1, win_body, init)
        o_s = acc_s / l_s
        o_w = acc_w / l_w
        for r in range(NSA_HPG):
            h = NSA_HPG * g + r
            rs = slice(r * TQ, (r + 1) * TQ)
            outs.append(gates[:, NSA_HEADS + h:NSA_HEADS + h + 1] * o_s[rs]
                        + gates[:, 2 * NSA_HEADS + h:2 * NSA_HEADS + h + 1] * o_w[rs])
    o_ref[...] = ocmp_ref[...] + jnp.concatenate(outs, axis=1)


def _pattn_call(p, mem, ocmp, tabs, *, nb, ln):
    nq = ln // TQ
    gw = NSA_GROUPS * NSA_DH
    tile = lambda w, cb: pl.BlockSpec((TQ, w), lambda b, i, cb=cb: (b * nq + i, cb))
    seq = lambda cb: pl.BlockSpec((ln, 4 * gw), lambda b, i, cb=cb: (b, cb))
    whole = lambda a: pl.BlockSpec(a.shape, lambda b, i: (0,) * a.ndim)
    kv_scr = pltpu.VMEM((NSA_GROUPS, ln, NSA_DH), BF16)
    return pl.pallas_call(
        _pattn_kernel,
        grid=(nb, nq),
        in_specs=[tile(NSA_WIDTH, P_QA // NSA_WIDTH), seq(P_KV4 // (4 * gw)), seq(P_WING // (4 * gw)),
                  tile(4 * gw, P_WING // (4 * gw)),
                  pl.BlockSpec((None, NSA_GROUPS, TQ, 128), lambda b, i: (b, 0, i, 0)),
                  tile(NSA_WIDTH, 0),
                  whole(tabs["sel"]), whole(tabs["win"]), whole(tabs["esel"])],
        out_specs=tile(NSA_WIDTH, 0),
        out_shape=jax.ShapeDtypeStruct((nb * ln, NSA_WIDTH), F32),
        scratch_shapes=[kv_scr, kv_scr, kv_scr, kv_scr, pltpu.VMEM((NSA_GROUPS, TQ, ln), F32)],
        compiler_params=pltpu.CompilerParams(dimension_semantics=("parallel", "arbitrary"),
                                             vmem_limit_bytes=VMEM_LIMIT_BYTES),
        name="nsa_sel_win",
    )(p, p, p, p, mem, ocmp, tabs["sel"], tabs["win"], tabs["esel"])


HG_SUB = 16


def _split3(x):
    h = x.astype(BF16)
    r = x - h.astype(F32)
    m = r.astype(BF16)
    return h, m, (r - m.astype(F32)).astype(BF16)


def _row_bcast(x, row, n):
    return jnp.broadcast_to(x[row:row + 1, :], (n, x.shape[1]))


def _hgrn_prompt_kernel(q_ref, f_ref, i_ref, g_ref, lb_ref, gain_ref, tri_ref, ecat_ref, o_ref, s_ref,
                        st_scr, *, n_chunks):
    step = pl.program_id(1)
    c = HG_CHUNK
    n_sub = c // HG_SUB

    @pl.when(step == 0)
    def _():
        st_scr[...] = jnp.zeros_like(st_scr)

    lb = lb_ref[...]
    rowi = lax.broadcasted_iota(jnp.int32, (c, HG_DK), 0)
    rmod = rowi % HG_SUB
    rsub = rowi // HG_SUB
    ti = lax.broadcasted_iota(jnp.int32, (c, c), 0)
    si = lax.broadcasted_iota(jnp.int32, (c, c), 1)
    diag_mask = (ti // HG_SUB == si // HG_SUB)

    for ch in range(n_chunks):
        r0 = ch * c
        fz = f_ref[pl.ds(r0, c), :]
        sg = jax.nn.sigmoid(fz)
        logf = jnp.log(lb + (1.0 - lb) * sg)
        kk = (1.0 - lb) * jax.nn.sigmoid(-fz)
        qq = _silu(q_ref[pl.ds(r0, c), :])
        vv = i_ref[pl.ds(r0, c), :]
        gg = g_ref[pl.ds(r0, c), :]
        h0, h1, h2 = _split3(logf)
        tri = tri_ref[...]
        b_all = _dot(tri, h0) + _dot(tri, h1) + _dot(tri, h2)
        outs = []
        for h in range(HG_HEADS):
            sl = slice(h * HG_DK, (h + 1) * HG_DK)
            b, q, k, v = b_all[:, sl], qq[:, sl], kk[:, sl], vv[:, sl]
            vb = v.astype(BF16)
            st = st_scr[h]
            o = _dot_nt((q * jnp.exp(b)).astype(BF16), st.astype(BF16))
            bstart = jnp.concatenate(
                [jnp.zeros((HG_SUB, HG_DK), F32)]
                + [_row_bcast(b, HG_SUB * i - 1, HG_SUB) for i in range(1, n_sub)], axis=0)
            bend = jnp.concatenate([_row_bcast(b, HG_SUB * (j + 1) - 1, HG_SUB) for j in range(n_sub)], axis=0)
            qh = q * jnp.exp(b - bstart)
            kh = k * jnp.exp(bend - b)
            lhs, rhs = [], []
            for j in range(n_sub - 1):
                bj = b[HG_SUB * (j + 1) - 1:HG_SUB * (j + 1), :]
                dj = jnp.exp(jnp.minimum(bstart - bj, 0.0))
                lhs.append(jnp.where(rsub > j, qh * dj, 0.0).astype(BF16))
                rhs.append(jnp.where(rsub == j, kh, 0.0).astype(BF16))
            a_off = _dot_nt(jnp.concatenate(lhs, axis=1), jnp.concatenate(rhs, axis=1))
            zs = []
            for s in range(HG_SUB):
                ks = jnp.concatenate([_row_bcast(k, HG_SUB * i + s, HG_SUB) for i in range(n_sub)], axis=0)
                bs = jnp.concatenate([_row_bcast(b, HG_SUB * i + s, HG_SUB) for i in range(n_sub)], axis=0)
                dec = jnp.where(rmod >= s, jnp.exp(jnp.minimum(b - bs, 0.0)), 0.0)
                zs.append((q * ks * dec).astype(BF16))
            a_diag = _dot(jnp.concatenate(zs, axis=1), ecat_ref[...])
            a = a_off + jnp.where(diag_mask, a_diag, 0.0)
            o = o + _dot(a.astype(BF16), vb)
            blast = b[c - 1:c, :]
            kdec = (k * jnp.exp(blast - b)).astype(BF16)
            st_scr[h] = st * jnp.exp(blast) + lax.dot_general(
                vb, kdec, (((0,), (0,)), ((), ())), preferred_element_type=F32)
            y = o * lax.rsqrt(jnp.mean(o * o, axis=-1, keepdims=True) + EPS) * gain_ref[...]
            outs.append(y * _silu(gg[:, sl]))
        o_ref[pl.ds(r0, c), :] = jnp.concatenate(outs, axis=1)

    @pl.when(step == pl.num_programs(1) - 1)
    def _():
        for h in range(HG_HEADS):
            s_ref[h] = st_scr[h].T


def _hgrn_prompt_call(p, lb_l, gain_l, *, nb, ln, tr):
    c = HG_CHUNK
    w = HG_HEADS * HG_DK
    steps = ln // tr
    tri = jnp.asarray(np.tril(np.ones((c, c), np.float32)), BF16)
    ecat = np.zeros((HG_SUB * HG_DK, c), np.float32)
    for s in range(HG_SUB):
        ecat[s * HG_DK:(s + 1) * HG_DK, s::HG_SUB] = 1.0
    col = lambda off: pl.BlockSpec((tr, w), lambda b, i, off=off: (b * steps + i, off // w))
    return pl.pallas_call(
        functools.partial(_hgrn_prompt_kernel, n_chunks=tr // c),
        grid=(nb, steps),
        in_specs=[col(P_QB), col(P_FB), col(P_IB), col(P_GB),
                  pl.BlockSpec((1, w), lambda b, i: (0, 0)),
                  pl.BlockSpec((1, HG_DV), lambda b, i: (0, 0)),
                  pl.BlockSpec((c, c), lambda b, i: (0, 0)),
                  pl.BlockSpec((HG_SUB * HG_DK, c), lambda b, i: (0, 0))],
        out_specs=[pl.BlockSpec((tr, w), lambda b, i: (b * steps + i, 0)),
                   pl.BlockSpec((None, HG_HEADS, HG_DK, HG_DV), lambda b, i: (b, 0, 0, 0))],
        out_shape=[jax.ShapeDtypeStruct((nb * ln, w), F32),
                   jax.ShapeDtypeStruct((nb, HG_HEADS, HG_DK, HG_DV), F32)],
        scratch_shapes=[pltpu.VMEM((HG_HEADS, HG_DV, HG_DK), F32)],
        compiler_params=pltpu.CompilerParams(dimension_semantics=("parallel", "arbitrary"),
                                             vmem_limit_bytes=VMEM_LIMIT_BYTES),
        name="hgrn_prompt",
    )(p, p, p, p, lb_l.reshape(1, w), gain_l.reshape(1, HG_DV), tri, jnp.asarray(ecat, BF16))


def _hgrn_decode_kernel(q_ref, f_ref, i_ref, g_ref, lb_ref, gain_ref, s0_ref, o_ref, s_ref, *, ls):
    lb = lb_ref[...]
    fz = f_ref[...]
    f = lb + (1.0 - lb) * jax.nn.sigmoid(fz)
    kk = (1.0 - lb) * jax.nn.sigmoid(-fz)
    qq = _silu(q_ref[...])
    vv = i_ref[...]
    gg = g_ref[...]
    pad = jnp.zeros((8 - ls, HG_DK), F32)
    tail = jnp.zeros((HG_DK - 24, HG_DK), F32)
    outs = []
    for h in range(HG_HEADS):
        sl = slice(h * HG_DK, (h + 1) * HG_DK)
        cols = jnp.concatenate([f[:, sl], pad, kk[:, sl], pad, qq[:, sl], pad, tail], axis=0).T
        st = s0_ref[h]
        rows = []
        for t in range(ls):
            st = cols[:, t:t + 1] * st + cols[:, 8 + t:9 + t] * vv[t:t + 1, sl]
            rows.append(jnp.sum(cols[:, 16 + t:17 + t] * st, axis=0, keepdims=True))
        s_ref[h] = st
        o = jnp.concatenate(rows, axis=0)
        y = o * lax.rsqrt(jnp.mean(o * o, axis=-1, keepdims=True) + EPS) * gain_ref[...]
        outs.append(y * _silu(gg[:, sl]))
    o_ref[...] = jnp.concatenate(outs, axis=1)


def _hgrn_decode_call(qb, fb, ib, gb, lb_l, gain_l, s0):
    n_seq, ls, w = qb.shape
    tok = pl.BlockSpec((None, ls, w), lambda s: (s, 0, 0))
    st = pl.BlockSpec((None, HG_HEADS, HG_DK, HG_DV), lambda s: (s, 0, 0, 0))
    return pl.pallas_call(
        functools.partial(_hgrn_decode_kernel, ls=ls),
        grid=(n_seq,),
        in_specs=[tok, tok, tok, tok,
                  pl.BlockSpec((1, w), lambda s: (0, 0)),
                  pl.BlockSpec((1, HG_DV), lambda s: (0, 0)),
                  st],
        out_specs=[tok, st],
        out_shape=[jax.ShapeDtypeStruct((n_seq, ls, w), F32),
                   jax.ShapeDtypeStruct((n_seq, HG_HEADS, HG_DK, HG_DV), F32)],
        compiler_params=pltpu.CompilerParams(dimension_semantics=("parallel",)),
        name="hgrn_decode",
    )(qb, fb, ib, gb, lb_l.reshape(1, w), gain_l.reshape(1, HG_DV), s0)


def _block_overlap(n_cmp, n_sel):
    c0 = np.arange(n_cmp) * CMP_STRIDE
    s0 = np.arange(n_sel) * SEL_LEN
    return ((c0[:, None] <= s0[None] + SEL_LEN - 1) & (c0[:, None] + CMP_LEN - 1 >= s0[None])).astype(np.float32)


def _nsa_mixer(q, kv_new, gates, past_kv, win_buf, w1, b1, w2, b2, rel_table):
    B, L = q.shape[:2]
    q_pos0 = 0 if past_kv is None else past_kv.shape[1]
    q_pos = q_pos0 + jnp.arange(L)
    full = kv_new[:, :, :4] if past_kv is None else jnp.concatenate([past_kv, kv_new[:, :, :4]], axis=1)
    T = full.shape[1]
    n_cmp = (T - CMP_LEN) // CMP_STRIDE + 1
    k_cmp = _compress(full[:, :, 0], w1[0], b1[0], w2[0], b2[0], n_cmp)
    v_cmp = _compress(full[:, :, 1], w1[1], b1[1], w2[1], b2[1], n_cmp)
    dist_c = q_pos[:, None] - (jnp.arange(n_cmp) * CMP_STRIDE + (CMP_LEN - 1))[None, :]
    p_cmp = _gqa_probs(q, k_cmp, dist_c >= 0, jnp.moveaxis(rel_table[_rel_bucket(dist_c)], -1, 0))
    o_cmp = _gqa_apply(p_cmp, v_cmp)
    n_sel = -(-T // SEL_LEN)
    overlap = jnp.asarray(_block_overlap(n_cmp, n_sel), dtype=p_cmp.dtype)
    score = jnp.einsum('bgrqc,cs->bgqs', p_cmp, overlap)
    blk = jnp.arange(n_sel)[None, :]
    cur = (q_pos // SEL_LEN)[:, None]
    forced = (blk == 0) | (blk == cur) | (blk == cur - 1)
    score = jnp.where(forced, jnp.inf, jnp.where(blk <= cur, score, -jnp.inf))
    n_top = min(SEL_TOP, n_sel)
    gt = score[..., :, None] > score[..., None, :]
    eq = (score[..., :, None] == score[..., None, :]) & (blk[0][:, None] < blk[0][None, :])
    rank = jnp.sum((gt | eq).astype(jnp.int32), axis=-2)
    member = rank < n_top
    pad = n_sel * SEL_LEN - T
    ks = jnp.pad(full[:, :, 2], ((0, 0), (0, pad), (0, 0), (0, 0)))
    vs = jnp.pad(full[:, :, 3], ((0, 0), (0, pad), (0, 0), (0, 0)))
    kpos = jnp.arange(n_sel * SEL_LEN)
    dist_s = q_pos[:, None] - kpos[None, :]
    mask_s = jnp.repeat(member, SEL_LEN, axis=-1) & (dist_s >= 0)[None, None]
    bias_s = jnp.moveaxis(rel_table[_rel_bucket(dist_s)], -1, 0)
    qg = q.reshape(B, L, NSA_GROUPS, NSA_HPG, NSA_DH)
    logits = jnp.einsum('bqgrd,bkgd->bgrqk', qg, ks, preferred_element_type=F32) * NSA_SCALE
    logits = logits + bias_s.reshape(NSA_GROUPS, NSA_HPG, L, -1)
    p_sel = _masked_softmax(logits, mask_s[:, :, None])
    o_sel = _gqa_apply(p_sel, vs)
    if win_buf is None:
        kpos_w = jnp.arange(L)
        dist = q_pos[:, None] - kpos_w[None, :]
        mask = (dist >= 0) & (dist < WINDOW)
        p_win = _gqa_probs(q, kv_new[:, :, 4], mask, jnp.moveaxis(rel_table[_rel_bucket(dist)], -1, 0))
        o_win = _gqa_apply(p_win, kv_new[:, :, 5])
        new_win = kv_new[:, -min(WINDOW, L):, 4:]
    else:
        wkv = jnp.concatenate([win_buf, kv_new[:, :, 4:]], axis=1)
        kpos_w = q_pos0 - win_buf.shape[1] + jnp.arange(wkv.shape[1])
        dist = q_pos[:, None] - kpos_w[None, :]
        mask = (dist >= 0) & (dist < WINDOW)
        p_win = _gqa_probs(q, wkv[:, :, 0], mask, jnp.moveaxis(rel_table[_rel_bucket(dist)], -1, 0))
        o_win = _gqa_apply(p_win, wkv[:, :, 1])
        new_win = wkv[:, -win_buf.shape[1]:]
    o = gates[:, :, 0, :, None] * o_cmp + gates[:, :, 1, :, None] * o_sel + gates[:, :, 2, :, None] * o_win
    return o.reshape(B, L, NSA_WIDTH), new_win


def _hgrn_scan(q, k, v, logf, s0):
    B, L, H, _ = q.shape
    DV = v.shape[-1]
    C = math.gcd(L, HG_CHUNK)
    nc = L // C

    def chunks(a):
        return a.astype(F32).reshape(B, nc, C, H, a.shape[-1]).transpose(1, 0, 3, 2, 4)

    causal = jnp.tril(jnp.ones((C, C), dtype=bool))[:, :, None]

    def step(s, inp):
        qc, kc, vc, lc = inp
        b = jnp.cumsum(lc, axis=2)
        decay = jnp.exp(jnp.where(causal, b[:, :, :, None, :] - b[:, :, None, :, :], -jnp.inf))
        a = jnp.einsum('bhtk,bhsk,bhtsk->bhts', qc, kc, decay)
        o = jnp.einsum('bhtk,bhkv->bhtv', qc * jnp.exp(b), s) + jnp.einsum('bhts,bhsv->bhtv', a, vc)
        b_last = b[:, :, -1:, :]
        s = jnp.exp(b_last[:, :, 0, :, None]) * s + jnp.einsum('bhsk,bhsv->bhkv', kc * jnp.exp(b_last - b), vc)
        return s, o

    s, o = lax.scan(step, s0.astype(F32), (chunks(q), chunks(k), chunks(v), chunks(logf)))
    return o.transpose(1, 0, 3, 2, 4).reshape(B, L, H, DV), s


def _hgrn2_mixer(q_raw, f_raw, i_raw, g_raw, lb, s0, gain):
    B, L = q_raw.shape[:2]
    q = jax.nn.silu(q_raw.reshape(B, L, HG_HEADS, HG_DK))
    fz = f_raw.reshape(B, L, HG_HEADS, HG_DK).astype(F32)
    logf = jnp.log(lb + (1 - lb) * jax.nn.sigmoid(fz))
    k = (1 - lb) * jax.nn.sigmoid(-fz)
    o, s = _hgrn_scan(q, k, i_raw.reshape(B, L, HG_HEADS, HG_DV), logf, s0)
    o = _rmsnorm(o.astype(q_raw.dtype), gain) * jax.nn.silu(g_raw.reshape(B, L, HG_HEADS, HG_DV))
    return o.reshape(B, L, HG_WIDTH), s


def _scaffold_step(x_prompt, x_sample, cache_nsa_kv, state_win_kv, state_hgrn, page_table, c_prompt, c_sample,
                   norm_g, ada_w, ada_b, ffn_w_in, ffn_w_out, w_in, cmp_w1, cmp_b1, cmp_w2, cmp_b2, rel_table,
                   hgrn_lb_logits, hgrn_norm_g, w_branch_nsa, w_branch_hgrn, w_out, final_g):
    depth = norm_g.shape[0]
    bp, lp, d = x_prompt.shape
    bs, ls, _ = x_sample.shape
    assert bs == 128, "sample rows are tiled per 128 sequences"

    lb = jnp.cumsum(jax.nn.softmax(hgrn_lb_logits.astype(F32), axis=0), axis=0)
    lb = (lb - lb[:1]).reshape(depth, HG_HEADS, HG_DK)

    col_idx, d_in = _proj_column_index(d)
    w_in_p = jnp.take(jnp.pad(w_in, ((0, 0), (0, 0), (0, 1))), col_idx, axis=2).astype(BF16)
    ffn_w_in_b = ffn_w_in.astype(BF16)
    ffn_w_out_b = ffn_w_out.astype(BF16)
    wpa_b = w_branch_nsa.astype(BF16)
    wpb_b = w_branch_hgrn.astype(BF16)
    wo_b = w_out.astype(BF16)

    mods = _mods_call(jnp.concatenate([c_sample, c_prompt], axis=0), ada_w, ada_b)

    n_seq, n_pages = page_table.shape
    page = cache_nsa_kv.shape[2]

    def trunk(x2, *, nb, ln, tm, seq_tiles, token_major, past_fn):
        kv_rows, win_bufs, hg_states = [], [], []

        def to_bl(a2):
            if token_major:
                return a2.reshape(ln, nb, a2.shape[-1]).transpose(1, 0, 2)
            return a2.reshape(nb, ln, a2.shape[-1])

        def to_t(a3):
            if token_major:
                return a3.transpose(1, 0, 2).reshape(ln * nb, a3.shape[-1])
            return a3.reshape(nb * ln, a3.shape[-1])

        for l in range(depth):
            past_kv, win_buf, s0 = past_fn(l)
            x2 = _ffn_call(x2, norm_g, mods, ffn_w_in_b, ffn_w_out_b, final_g, l=l, s=0, sub=0, tm=tm, tf=256,
                           seq_tiles=seq_tiles, final=False)
            p = _proj_call(x2, norm_g, mods, w_in_p, l=l, tm=tm, tn=512, seq_tiles=seq_tiles)
            p3 = to_bl(p)
            qa = p3[..., P_QA:P_QA + 512]
            kva = p3[..., P_KV4:P_KV4 + 768]
            ga = p3[..., P_WING + 256:P_WING + 256 + 3 * NSA_HEADS]
            qb, fb, ib, gb = (p3[..., o:o + 512] for o in (P_QB, P_FB, P_IB, P_GB))
            kv_new = kva.reshape(nb, ln, 6, NSA_GROUPS, NSA_DH)
            o_a, new_win = _nsa_mixer(qa.reshape(nb, ln, NSA_HEADS, NSA_DH), kv_new,
                                      jax.nn.sigmoid(ga.reshape(nb, ln, 3, NSA_HEADS)), past_kv, win_buf,
                                      cmp_w1[l], cmp_b1[l], cmp_w2[l], cmp_b2[l], rel_table)
            o_b, s_new = _hgrn2_mixer(qb, fb, ib, gb, lb[l], s0, hgrn_norm_g[l])
            x2 = _merge_call(x2, to_t(o_a), to_t(o_b), p, mods, wpa_b, wpb_b, wo_b, l=l, tm=tm, seq_tiles=seq_tiles)
            x2 = _ffn_call(x2, norm_g, mods, ffn_w_in_b, ffn_w_out_b, final_g, l=l, s=1, sub=2, tm=tm, tf=256,
                           seq_tiles=seq_tiles, final=(l == depth - 1))
            kv_rows.append(kv_new[:, :, :4])
            win_bufs.append(new_win)
            hg_states.append(s_new)
        return to_bl(x2), jnp.stack(kv_rows), jnp.stack(win_bufs), jnp.stack(hg_states)

    def prompt_past(l):
        return None, None, jnp.zeros((bp, HG_HEADS, HG_DK, HG_DV), F32)

    def sample_past(l):
        past = cache_nsa_kv[l][page_table].reshape(n_seq, n_pages * page, 4, NSA_GROUPS, NSA_DH)
        return past, state_win_kv[l], state_hgrn[l]

    tm_p = 1024 if lp % 1024 == 0 else lp
    y_p, kv_p, win_p, hg_p = trunk(x_prompt.reshape(bp * lp, d), nb=bp, ln=lp, tm=tm_p,
                                   seq_tiles=(bs, lp // tm_p), token_major=False, past_fn=prompt_past)
    xs_tm = x_sample.transpose(1, 0, 2).reshape(ls * bs, d)
    y_s, kv_s, win_s, hg_s = trunk(xs_tm, nb=bs, ln=ls, tm=ls * bs, seq_tiles=None, token_major=True,
                                   past_fn=sample_past)
    return (y_p, y_s, kv_p, kv_s, win_p, win_s, hg_p, hg_s)


def kernel(x_prompt, x_sample, cache_nsa_kv, state_win_kv, state_hgrn, page_table, c_prompt, c_sample,
           norm_g, ada_w, ada_b, ffn_w_in, ffn_w_out, w_in, cmp_w1, cmp_b1, cmp_w2, cmp_b2, rel_table,
           hgrn_lb_logits, hgrn_norm_g, w_branch_nsa, w_branch_hgrn, w_out, final_g):
    depth = norm_g.shape[0]
    bp, lp, d = x_prompt.shape
    bs, ls, _ = x_sample.shape
    n_seq, n_pages = page_table.shape
    n_pool, page = cache_nsa_kv.shape[1:3]
    past = n_pages * page
    w_len = state_win_kv.shape[2]
    gw = NSA_GROUPS * NSA_DH
    assert bs == 128 and n_seq == bs, "sample rows are tiled per 128 sequences"
    assert lp % TKS == 0 and lp >= WINDOW and (lp // CMP_STRIDE) % 128 == 0
    n_cmp_s = (past + ls - CMP_LEN) // CMP_STRIDE + 1
    n_sel_s = -(-(past + ls) // SEL_LEN)
    assert (n_cmp_s - 1) * CMP_STRIDE + CMP_LEN <= past and (past // CMP_STRIDE) % 128 == 0
    assert past % SEL_LEN == 0 and ls <= 8 and past // SEL_LEN == (past + ls - 1) // SEL_LEN

    lb = jnp.cumsum(jax.nn.softmax(hgrn_lb_logits.astype(F32), axis=0), axis=0)
    lb = lb - lb[:1]

    col_idx, _ = _proj_column_index(d)
    w_in_p = jnp.take(jnp.pad(w_in, ((0, 0), (0, 0), (0, 1))), col_idx, axis=2).astype(BF16)
    ffn_w_in_b = ffn_w_in.astype(BF16)
    ffn_w_out_b = ffn_w_out.astype(BF16)
    wpa_b = w_branch_nsa.astype(BF16)
    wpb_b = w_branch_hgrn.astype(BF16)
    wo_b = w_out.astype(BF16)
    cw = _compress_weights(cmp_w1, cmp_b1, cmp_w2, cmp_b2)
    tabs_p = _prompt_tables(rel_table, ln=lp)
    tabs_s = _decode_tables(rel_table, past=past, ls=ls, n_cmp=n_cmp_s, n_sel=n_sel_s, w_len=w_len)
    cache4 = cache_nsa_kv.reshape(depth, n_pool, page, 4 * gw)

    mods = _mods_call(jnp.concatenate([c_sample, c_prompt], axis=0), ada_w, ada_b)

    def sublayers(x2, l, tm, seq_tiles, mixers):
        x2 = _ffn_call(x2, norm_g, mods, ffn_w_in_b, ffn_w_out_b, final_g, l=l, s=0, sub=0, tm=tm, tf=256,
                       seq_tiles=seq_tiles, final=False)
        p = _proj_call(x2, norm_g, mods, w_in_p, l=l, tm=tm, tn=512, seq_tiles=seq_tiles)
        o_a, o_b, extras = mixers(p)
        x2 = _merge_call(x2, o_a, o_b, p, mods, wpa_b, wpb_b, wo_b, l=l, tm=tm, seq_tiles=seq_tiles)
        x2 = _ffn_call(x2, norm_g, mods, ffn_w_in_b, ffn_w_out_b, final_g, l=l, s=1, sub=2, tm=tm, tf=256,
                       seq_tiles=seq_tiles, final=(l == depth - 1))
        return x2, extras

    tm_p = 1024 if lp % 1024 == 0 else lp
    x2 = x_prompt.reshape(bp * lp, d)
    kv_p, win_p, hg_p = [], [], []
    for l in range(depth):
        def prompt_mixers(p, l=l):
            ck, cv = _pcompress_call(p, cw, l=l, nb=bp, ln=lp)
            ocmp, mem = _pcmp_call(p, ck, cv, tabs_p, nb=bp, ln=lp)
            o_a = _pattn_call(p, mem, ocmp, tabs_p, nb=bp, ln=lp)
            o_b, s_new = _hgrn_prompt_call(p, lb[l], hgrn_norm_g[l], nb=bp, ln=lp, tr=128)
            p3 = p.reshape(bp, lp, P_TOTAL)
            kv4 = p3[:, :, P_KV4:P_KV4 + 4 * gw].reshape(bp, lp, 4, NSA_GROUPS, NSA_DH)
            nwin = p3[:, lp - min(WINDOW, lp):, P_WING:P_WING + 2 * gw].reshape(bp, -1, 2, NSA_GROUPS, NSA_DH)
            return o_a, o_b, (kv4, nwin, s_new)

        x2, (kv4, nwin, s_new) = sublayers(x2, l, tm_p, (bs, lp // tm_p), prompt_mixers)
        kv_p.append(kv4)
        win_p.append(nwin)
        hg_p.append(s_new)
    y_p = x2.reshape(bp, lp, d)

    x2 = x_sample.transpose(1, 0, 2).reshape(ls * bs, d)
    kv_s, win_s, hg_s = [], [], []
    for l in range(depth):
        def sample_mixers(p, l=l):
            ps = p.reshape(ls, bs, P_TOTAL).transpose(1, 0, 2)
            q_s = ps[:, :, P_QA:P_QA + NSA_WIDTH].reshape(bs, ls, NSA_GROUPS, NSA_HPG, NSA_DH)
            q_s = q_s.transpose(0, 2, 3, 1, 4).reshape(bs, NSA_GROUPS, NSA_HPG * ls, NSA_DH)
            ga = ps[:, :, P_WING + 2 * gw:P_WING + 2 * gw + 3 * NSA_HEADS].reshape(bs, ls, 3, NSA_GROUPS, NSA_HPG)
            ga = ga.transpose(0, 3, 4, 1, 2).reshape(bs, NSA_GROUPS, NSA_HPG * ls, 3)
            kvn = ps[:, :, P_KV4:P_KV4 + 4 * gw]
            winn = ps[:, :, P_WING:P_WING + 2 * gw]
            o, nwin = _decode_call(cache4, page_table, q_s, kvn, winn, state_win_kv[l].reshape(bs, w_len, 2 * gw),
                                   ga, cw, tabs_s, l=l, past=past, n_sel=n_sel_s)
            o_a = o.reshape(bs, NSA_GROUPS, NSA_HPG, ls, NSA_DH).transpose(3, 0, 1, 2, 4).reshape(ls * bs, NSA_WIDTH)
            o_b, s_new = _hgrn_decode_call(ps[:, :, P_QB:P_QB + 512], ps[:, :, P_FB:P_FB + 512],
                                           ps[:, :, P_IB:P_IB + 512], ps[:, :, P_GB:P_GB + 512],
                                           lb[l], hgrn_norm_g[l], state_hgrn[l])
            o_b = o_b.transpose(1, 0, 2).reshape(ls * bs, HG_WIDTH)
            return o_a, o_b, (kvn.reshape(bs, ls, 4, NSA_GROUPS, NSA_DH),
                              nwin.reshape(bs, w_len, 2, NSA_GROUPS, NSA_DH), s_new)

        x2, (kv4, nwin, s_new) = sublayers(x2, l, ls * bs, None, sample_mixers)
        kv_s.append(kv4)
        win_s.append(nwin)
        hg_s.append(s_new)
    y_s = x2.reshape(ls, bs, d).transpose(1, 0, 2)

    return (y_p, y_s, jnp.stack(kv_p), jnp.stack(kv_s), jnp.stack(win_p), jnp.stack(win_s),
            jnp.stack(hg_p), jnp.stack(hg_s))
```

```python
import functools
import math

import jax
import jax.numpy as jnp
import numpy as np
from jax import lax
from jax.experimental import pallas as pl
from jax.experimental.pallas import tpu as pltpu

NSA_HEADS = 8
NSA_GROUPS = 2
NSA_HPG = NSA_HEADS // NSA_GROUPS
NSA_DH = 64
NSA_WIDTH = NSA_HEADS * NSA_DH
NSA_SCALE = NSA_DH ** -0.5
CMP_LEN = 32
CMP_STRIDE = 16
CMP_RATIO = CMP_LEN // CMP_STRIDE
CMP_HID = 128
SEL_LEN = 64
SEL_TOP = 16
WINDOW = 512
HG_HEADS = 4
HG_DK = 128
HG_DV = 128
HG_WIDTH = HG_HEADS * HG_DV
HG_CHUNK = 64
REL_BUCKETS = 32
REL_MAX_DIST = 128
EPS = 1e-6

BF16 = jnp.bfloat16
F32 = jnp.float32

VMEM_LIMIT_BYTES = 56 * 1024 * 1024

P_MA, P_MB, P_QA, P_KV4, P_WING, P_QB, P_FB, P_IB, P_GB, P_TOTAL = (
    0, 1024, 2048, 2560, 3072, 3584, 4096, 4608, 5120, 5632)


def _permute_w_in(w_in):
    d_model = w_in.shape[1]
    sizes = (NSA_WIDTH, 6 * NSA_GROUPS * NSA_DH, 3 * NSA_HEADS, HG_HEADS * HG_DK, HG_HEADS * HG_DK,
             HG_WIDTH, HG_WIDTH, d_model, d_model)
    offs = np.concatenate([[0], np.cumsum(sizes)])
    qa, kva, ga, qb, fb, ib, gb, ma, mb = [w_in[:, :, int(offs[i]):int(offs[i + 1])] for i in range(9)]
    pad = jnp.zeros(w_in.shape[:2] + (P_QB - P_WING - kva.shape[2] + 4 * NSA_GROUPS * NSA_DH - ga.shape[2],), w_in.dtype)
    out = jnp.concatenate([ma, mb, qa, kva, ga, pad, qb, fb, ib, gb], axis=2)
    assert out.shape[2] == P_TOTAL and d_model == P_MB - P_MA
    return out.astype(BF16)


def _mods_kernel(c_ref, w_ref, b_ref, o_ref):
    c = c_ref[...]
    sc = (c * jax.nn.sigmoid(c)).astype(BF16)
    o_ref[...] = jnp.dot(sc, w_ref[...].astype(BF16), preferred_element_type=F32) + b_ref[...]


def _mods_call(c_all, ada_w, ada_b):
    depth, d, nd = ada_w.shape
    n = c_all.shape[0]
    b3 = ada_b.reshape(depth, 1, nd)
    return pl.pallas_call(
        _mods_kernel,
        grid=(depth, nd // d),
        in_specs=[
            pl.BlockSpec((n, d), lambda l, j: (0, 0)),
            pl.BlockSpec((None, d, d), lambda l, j: (l, 0, j)),
            pl.BlockSpec((None, 1, d), lambda l, j: (l, 0, j)),
        ],
        out_specs=pl.BlockSpec((None, None, n, d), lambda l, j: (l, j, 0, 0)),
        out_shape=jax.ShapeDtypeStruct((depth, nd // d, n, d), F32),
        compiler_params=pltpu.CompilerParams(dimension_semantics=("arbitrary", "arbitrary")),
        name="adaln_mods",
    )(c_all, ada_w, b3)


def _rows(m, tm):
    r = m.shape[0]
    if r == 1 or r == tm:
        return m
    return jnp.concatenate([m] * (tm // r), axis=0)


def _norm_mod(x, g, shift, scale):
    y = x * lax.rsqrt(jnp.mean(x * x, axis=-1, keepdims=True) + EPS) * g
    return y * (1.0 + scale) + shift


def _mod_spec(mods, l, k, tm, seq_tiles):
    if seq_tiles is not None:
        row0, tiles_per_seq = seq_tiles
        d = mods.shape[-1]
        m5 = mods.reshape(mods.shape[0], mods.shape[1], mods.shape[2], 1, d)
        return m5, pl.BlockSpec((None, None, None, 1, d),
                                lambda i, j, l=l, k=k: (l, k, row0 + i // tiles_per_seq, 0, 0))
    d = mods.shape[-1]
    return mods, pl.BlockSpec((None, None, 128, d), lambda i, j, l=l, k=k: (l, k, 0, 0))


def _ffn_kernel(x_ref, g_ref, sh_ref, sc_ref, gt_ref, wa_ref, wb_ref, wo_ref, fg_ref, o_ref,
                n_scr, acc_scr, *, nf, final):
    j = pl.program_id(1)
    tm = x_ref.shape[0]

    @pl.when(j == 0)
    def _():
        n = _norm_mod(x_ref[...], g_ref[...], _rows(sh_ref[...], tm), _rows(sc_ref[...], tm))
        n_scr[...] = n.astype(BF16)
        acc_scr[...] = jnp.zeros_like(acc_scr)

    n = n_scr[...]
    a = jnp.dot(n, wa_ref[...], preferred_element_type=F32)
    b = jnp.dot(n, wb_ref[...], preferred_element_type=F32)
    h = (a * jax.nn.sigmoid(a) * b).astype(BF16)
    acc_scr[...] += jnp.dot(h, wo_ref[...], preferred_element_type=F32)

    @pl.when(j == nf - 1)
    def _():
        y = x_ref[...] + 0.5 * _rows(gt_ref[...], tm) * acc_scr[...]
        if final:
            y = y * lax.rsqrt(jnp.mean(y * y, axis=-1, keepdims=True) + EPS) * fg_ref[...]
        o_ref[...] = y


def _ffn_call(x, norm_g, mods, ffn_w_in, ffn_w_out, final_g, *, l, s, sub, tm, tf, seq_tiles, final):
    t, d = x.shape
    d_ff = ffn_w_out.shape[2]
    nf = d_ff // tf
    g2 = norm_g[l, sub].reshape(1, d)
    fg2 = final_g.reshape(1, d)
    m_sh, sp_sh = _mod_spec(mods, l, 3 * sub, tm, seq_tiles)
    m_sc, sp_sc = _mod_spec(mods, l, 3 * sub + 1, tm, seq_tiles)
    m_gt, sp_gt = _mod_spec(mods, l, 3 * sub + 2, tm, seq_tiles)
    return pl.pallas_call(
        functools.partial(_ffn_kernel, nf=nf, final=final),
        grid=(t // tm, nf),
        in_specs=[
            pl.BlockSpec((tm, d), lambda i, j: (i, 0)),
            pl.BlockSpec((1, d), lambda i, j: (0, 0)),
            sp_sh, sp_sc, sp_gt,
            pl.BlockSpec((None, None, d, tf), lambda i, j: (l, s, 0, j)),
            pl.BlockSpec((None, None, d, tf), lambda i, j: (l, s, 0, nf + j)),
            pl.BlockSpec((None, None, tf, d), lambda i, j: (l, s, j, 0)),
            pl.BlockSpec((1, d), lambda i, j: (0, 0)),
        ],
        out_specs=pl.BlockSpec((tm, d), lambda i, j: (i, 0)),
        out_shape=jax.ShapeDtypeStruct((t, d), F32),
        scratch_shapes=[pltpu.VMEM((tm, d), BF16), pltpu.VMEM((tm, d), F32)],
        compiler_params=pltpu.CompilerParams(
            dimension_semantics=("parallel", "arbitrary"), vmem_limit_bytes=VMEM_LIMIT_BYTES),
        name="ffn",
    )(x, g2, m_sh, m_sc, m_gt, ffn_w_in, ffn_w_in, ffn_w_out, fg2)


def _proj_kernel(x_ref, g_ref, sh_ref, sc_ref, w_ref, o_ref, n_scr):
    j = pl.program_id(1)
    tm = x_ref.shape[0]

    @pl.when(j == 0)
    def _():
        n = _norm_mod(x_ref[...], g_ref[...], _rows(sh_ref[...], tm), _rows(sc_ref[...], tm))
        n_scr[...] = n.astype(BF16)

    o_ref[...] = jnp.dot(n_scr[...], w_ref[...], preferred_element_type=F32)


def _proj_call(x, norm_g, mods, w_in_p, *, l, tm, tn, seq_tiles):
    t, d = x.shape
    n_out = w_in_p.shape[2]
    g2 = norm_g[l, 1].reshape(1, d)
    m_sh, sp_sh = _mod_spec(mods, l, 3, tm, seq_tiles)
    m_sc, sp_sc = _mod_spec(mods, l, 4, tm, seq_tiles)
    return pl.pallas_call(
        _proj_kernel,
        grid=(t // tm, n_out // tn),
        in_specs=[
            pl.BlockSpec((tm, d), lambda i, j: (i, 0)),
            pl.BlockSpec((1, d), lambda i, j: (0, 0)),
            sp_sh, sp_sc,
            pl.BlockSpec((None, d, tn), lambda i, j: (l, 0, j)),
        ],
        out_specs=pl.BlockSpec((tm, tn), lambda i, j: (i, j)),
        out_shape=jax.ShapeDtypeStruct((t, n_out), F32),
        scratch_shapes=[pltpu.VMEM((tm, d), BF16)],
        compiler_params=pltpu.CompilerParams(
            dimension_semantics=("parallel", "arbitrary"), vmem_limit_bytes=VMEM_LIMIT_BYTES),
        name="in_proj",
    )(x, g2, m_sh, m_sc, w_in_p)


def _merge_kernel(x_ref, oa_ref, ob_ref, ma_ref, mb_ref, gt_ref, wpa_ref, wpb_ref, wo_ref, o_ref):
    tm = x_ref.shape[0]
    ya = jnp.dot(oa_ref[...].astype(BF16), wpa_ref[...], preferred_element_type=F32)
    yb = jnp.dot(ob_ref[...].astype(BF16), wpb_ref[...], preferred_element_type=F32)
    merged = jax.nn.sigmoid(ma_ref[...]) * ya + jax.nn.sigmoid(mb_ref[...]) * yb
    y = jnp.dot(merged.astype(BF16), wo_ref[...], preferred_element_type=F32)
    o_ref[...] = x_ref[...] + _rows(gt_ref[...], tm) * y


def _merge_call(x, o_a, o_b, p, mods, wpa, wpb, wo, *, l, tm, seq_tiles):
    t, d = x.shape
    m_gt, sp_gt = _mod_spec(mods, l, 5, tm, seq_tiles)
    one = lambda i, j: (i, 0)
    return pl.pallas_call(
        _merge_kernel,
        grid=(t // tm, 1),
        in_specs=[
            pl.BlockSpec((tm, d), one),
            pl.BlockSpec((tm, NSA_WIDTH), one),
            pl.BlockSpec((tm, HG_WIDTH), one),
            pl.BlockSpec((tm, d), lambda i, j: (i, P_MA // d)),
            pl.BlockSpec((tm, d), lambda i, j: (i, P_MB // d)),
            sp_gt,
            pl.BlockSpec((None, NSA_WIDTH, d), lambda i, j: (l, 0, 0)),
            pl.BlockSpec((None, HG_WIDTH, d), lambda i, j: (l, 0, 0)),
            pl.BlockSpec((None, d, d), lambda i, j: (l, 0, 0)),
        ],
        out_specs=pl.BlockSpec((tm, d), one),
        out_shape=jax.ShapeDtypeStruct((t, d), F32),
        compiler_params=pltpu.CompilerParams(
            dimension_semantics=("parallel", "arbitrary"), vmem_limit_bytes=VMEM_LIMIT_BYTES),
        name="merge_out",
    )(x, o_a, o_b, p, p, m_gt, wpa, wpb, wo)


NEG = -1e30


def _bucket_thresholds():
    n = np.arange(0, 4 * REL_MAX_DIST)
    exact = REL_BUCKETS // 2
    large = exact + np.floor(np.log(np.maximum(n, 1) / exact) / math.log(REL_MAX_DIST / exact)
                             * (REL_BUCKETS - exact)).astype(np.int64)
    bucket = np.where(n < exact, n, np.minimum(large, REL_BUCKETS - 1))
    return [int(np.min(n[bucket >= k])) for k in range(REL_BUCKETS)]


_THRESHOLDS = _bucket_thresholds()


def _bias_kernel(tab_ref, a_ref, b_ref, o_ref, *, hi):
    dist = a_ref[...] - b_ref[...]
    tab = tab_ref[...]
    v = jnp.broadcast_to(tab[:, REL_BUCKETS - 1:REL_BUCKETS], dist.shape)
    for k in range(REL_BUCKETS - 2, -1, -1):
        v = jnp.where(dist < _THRESHOLDS[k + 1], tab[:, k:k + 1], v)
    o_ref[...] = jnp.where((dist >= 0) & (dist < hi), v, NEG)


def _bias_table(rel_table, a, b, head, hi):
    r, c = a.shape[0], b.shape[0]
    tr = min(r, 512)
    assert r % tr == 0
    tab = jnp.take(rel_table, jnp.asarray(head, jnp.int32), axis=1).T
    return pl.pallas_call(
        functools.partial(_bias_kernel, hi=hi),
        grid=(r // tr,),
        in_specs=[pl.BlockSpec((tr, REL_BUCKETS), lambda i: (i, 0)),
                  pl.BlockSpec((tr, 1), lambda i: (i, 0)),
                  pl.BlockSpec((1, c), lambda i: (0, 0))],
        out_specs=pl.BlockSpec((tr, c), lambda i: (i, 0)),
        out_shape=jax.ShapeDtypeStruct((r, c), F32),
        name="rel_bias_table",
    )(tab, jnp.asarray(a, jnp.int32).reshape(r, 1), jnp.asarray(b, jnp.int32).reshape(1, c))


def _bias_cols_kernel(tab_ref, a_ref, b_ref, sub_ref, o_ref, *, hi):
    dist = a_ref[...] - b_ref[...]
    tab = tab_ref[...]
    v = jnp.broadcast_to(tab[REL_BUCKETS - 1:REL_BUCKETS, :], dist.shape)
    for k in range(REL_BUCKETS - 2, -1, -1):
        v = jnp.where(dist < _THRESHOLDS[k + 1], tab[k:k + 1, :], v)
    o_ref[...] = jnp.where((dist >= 0) & (dist < hi), v - sub_ref[...], NEG)


def _bias_table_cols(rel_table, a, b, head, sub, hi):
    r, c = a.shape[0], b.shape[0]
    tr = min(r, 256)
    assert r % tr == 0
    tab = jnp.take(rel_table, jnp.asarray(head, jnp.int32), axis=1)
    return pl.pallas_call(
        functools.partial(_bias_cols_kernel, hi=hi),
        grid=(r // tr,),
        in_specs=[pl.BlockSpec((REL_BUCKETS, c), lambda i: (0, 0)),
                  pl.BlockSpec((tr, 1), lambda i: (i, 0)),
                  pl.BlockSpec((1, c), lambda i: (0, 0)),
                  pl.BlockSpec((1, c), lambda i: (0, 0))],
        out_specs=pl.BlockSpec((tr, c), lambda i: (i, 0)),
        out_shape=jax.ShapeDtypeStruct((r, c), F32),
        name="rel_bias_table_t",
    )(tab, jnp.asarray(a, jnp.int32).reshape(r, 1), jnp.asarray(b, jnp.int32).reshape(1, c), sub.reshape(1, c))


BIG = 1 << 30


def _dot_nt(a, b):
    return lax.dot_general(a, b, (((1,), (1,)), ((), ())), preferred_element_type=F32)


def _dot(a, b):
    return jnp.dot(a, b, preferred_element_type=F32)


def _silu(x):
    return x * jax.nn.sigmoid(x)


def _block_overlap(n_cmp, n_sel):
    c0 = np.arange(n_cmp) * CMP_STRIDE
    s0 = np.arange(n_sel) * SEL_LEN
    return ((c0[:, None] <= s0[None] + SEL_LEN - 1) & (c0[:, None] + CMP_LEN - 1 >= s0[None])).astype(np.float32)


def _compress_weights(cmp_w1, cmp_b1, cmp_w2, cmp_b2):
    eye = jnp.eye(NSA_GROUPS, dtype=F32)
    w1 = jnp.einsum('lkrjde,gh->lkjgdrhe', cmp_w1, eye)
    depth = cmp_w1.shape[0]
    w1 = w1.reshape(depth, 2, CMP_STRIDE * NSA_GROUPS * NSA_DH, CMP_RATIO * NSA_GROUPS * CMP_HID).astype(BF16)
    w2 = jnp.einsum('lked,gh->lkgehd', cmp_w2, eye).reshape(depth, 2, NSA_GROUPS * CMP_HID, NSA_GROUPS * NSA_DH)
    b1 = jnp.tile(cmp_b1, (1, 1, NSA_GROUPS)).reshape(depth, 2, 1, NSA_GROUPS * CMP_HID)
    b2 = jnp.tile(cmp_b2, (1, 1, NSA_GROUPS)).reshape(depth, 2, 1, NSA_GROUPS * NSA_DH)
    return w1, b1, w2.astype(BF16), b2


def _compress_rows(load_piece, n_sub, w1, b1, w2, b2):
    x = jnp.concatenate([load_piece(j).astype(BF16) for j in range(CMP_STRIDE)], axis=1)
    hr = _dot(x, w1)
    half = NSA_GROUPS * CMP_HID
    h = hr[:, :half] + pltpu.roll(hr[:, half:], n_sub - 1, 0) + b1
    return _dot(_silu(h).astype(BF16), w2) + b2


def _topk_member_rows(sc, n_sel, n_top):
    blk = lax.broadcasted_iota(jnp.int32, sc.shape, 0)
    cnt = jnp.zeros(sc.shape, jnp.int32)
    for i in range(n_sel):
        si = sc[i:i + 1, :]
        ahead = (si > sc) | ((si == sc) & (blk > i))
        cnt = cnt + ahead.astype(jnp.int32)
    return ((cnt < n_top) & (blk < n_sel)).astype(F32)


def _topk_member_lanes(sc, n_sel, n_top):
    blk = lax.broadcasted_iota(jnp.int32, sc.shape, 1)
    cnt = jnp.zeros(sc.shape, jnp.int32)
    for i in range(n_sel):
        si = sc[:, i:i + 1]
        ahead = (si > sc) | ((si == sc) & (blk > i))
        cnt = cnt + ahead.astype(jnp.int32)
    return ((cnt < n_top) & (blk < n_sel)).astype(F32)


def _softmax_rows(lg):
    m = jnp.max(lg, axis=-1, keepdims=True)
    e = jnp.exp(lg - m)
    s = jnp.sum(e, axis=-1, keepdims=True)
    return jnp.where(m > 0.5 * NEG, e / s, 0.0)


def _decode_kernel(pt_ref, cache_ref, q_ref, kvn_ref, winn_ref, winbuf_ref, gate_ref,
                   w1_ref, b1_ref, w2_ref, b2_ref, bcmp_ref, bsel_ref, bwin_ref, ovl_ref, esel_ref,
                   o_ref, newwin_ref, buf, wbuf, cbuf, sem, *, l, n_pages, page, ls, past, n_sel, n_top):
    s = pl.program_id(0)
    n_seq = pl.num_programs(0)
    slot = s % 2
    t_cols = buf.shape[2]
    w_cols = wbuf.shape[1]
    w_len = winbuf_ref.shape[1]
    gw = NSA_GROUPS * NSA_DH

    def page_copy(seq, sl, pg):
        return pltpu.make_async_copy(cache_ref.at[l, pt_ref[seq, pg]],
                                     buf.at[sl, :, pl.ds(pg * page, page)], sem.at[sl])

    def start_all(seq, sl):
        for pg in range(n_pages):
            page_copy(seq, sl, pg).start()

    @pl.when(s == 0)
    def _():
        start_all(0, 0)
        for sl in range(2):
            buf[sl, :, pl.ds(past, t_cols - past)] = jnp.zeros((buf.shape[1], t_cols - past), F32)
        wbuf[:, pl.ds(w_len, w_cols - w_len)] = jnp.zeros((wbuf.shape[0], w_cols - w_len), F32)

    @pl.when(s + 1 < n_seq)
    def _():
        start_all(s + 1, 1 - slot)

    for pg in range(n_pages):
        page_copy(s, slot, pg).wait()

    xb = buf.at[slot]
    xb[:, pl.ds(past, ls)] = kvn_ref[...]
    wbuf[:, pl.ds(0, w_len)] = winbuf_ref[...]
    wbuf[:, pl.ds(w_len, ls)] = winn_ref[...]
    newwin_ref[...] = pltpu.roll(wbuf[...], w_cols - ls, 1)[:, 0:w_len]

    n_sub = past // CMP_STRIDE
    summ = []
    for k in range(2):
        cbuf[k] = xb[k * gw:(k + 1) * gw, pl.ds(0, past)].T
        summ.append(_compress_rows(
            lambda j, k=k: cbuf[k, pl.ds(j, n_sub, stride=CMP_STRIDE), :],
            n_sub, w1_ref[k], b1_ref[k], w2_ref[k], b2_ref[k]))
    ck, cv = summ

    tokpos = past + lax.broadcasted_iota(jnp.int32, (ls, 128), 0)
    blk = lax.broadcasted_iota(jnp.int32, (ls, 128), 1)
    cur = tokpos // SEL_LEN
    forced = (blk == 0) | (blk == cur) | (blk == cur - 1)
    for g in range(NSA_GROUPS):
        lo, hi = g * NSA_DH, (g + 1) * NSA_DH
        qg = (q_ref[g] * NSA_SCALE).astype(BF16)
        gt = jax.nn.sigmoid(gate_ref[g])
        p = _softmax_rows(_dot_nt(qg, ck[:, lo:hi].astype(BF16)) + bcmp_ref[g])
        pb = p.astype(BF16)
        o_cmp = _dot(pb, cv[:, lo:hi].astype(BF16))
        ps = _dot(pb, ovl_ref[...])
        score = ps[0:ls]
        for r in range(1, NSA_HPG):
            score = score + ps[r * ls:(r + 1) * ls]
        sc = jnp.where(forced, jnp.inf, jnp.where(blk <= cur, score, -jnp.inf))
        member = _topk_member_lanes(sc, n_sel, n_top)
        mem = jnp.concatenate([member] * NSA_HPG, axis=0).astype(BF16)
        addmask = (_dot(mem, esel_ref[...]) - 1.0) * (-NEG)
        ksel_t = xb[2 * gw + lo:2 * gw + hi, :].astype(BF16)
        vsel_t = xb[3 * gw + lo:3 * gw + hi, :].astype(BF16)
        p = _softmax_rows(_dot(qg, ksel_t) + bsel_ref[g] + addmask)
        o_sel = _dot_nt(p.astype(BF16), vsel_t)
        kwin_t = wbuf[lo:hi, :].astype(BF16)
        vwin_t = wbuf[gw + lo:gw + hi, :].astype(BF16)
        p = _softmax_rows(_dot(qg, kwin_t) + bwin_ref[g])
        o_win = _dot_nt(p.astype(BF16), vwin_t)
        o_ref[g] = gt[:, 0:1] * o_cmp + gt[:, 1:2] * o_sel + gt[:, 2:3] * o_win


def _decode_call(cache_t, page_table, q_s, kvn_t, winn_t, winbuf_t, gates, cw, tabs, *, l, past, n_sel):
    n_seq, n_pages = page_table.shape
    page = cache_t.shape[3]
    ls = kvn_t.shape[2]
    rows = NSA_HPG * ls
    w_len = winbuf_t.shape[3]
    t_cols = tabs["sel"].shape[-1]
    w_cols = tabs["win"].shape[-1]
    w1, b1, w2, b2 = cw
    gw = NSA_GROUPS * NSA_DH
    n_top = min(SEL_TOP, n_sel)
    kern = functools.partial(_decode_kernel, l=l, n_pages=n_pages, page=page, ls=ls, past=past,
                             n_sel=n_sel, n_top=n_top)
    whole = lambda shape: pl.BlockSpec(shape, lambda s, pt: (0,) * len(shape))
    lay = lambda a: pl.BlockSpec((None,) + a.shape[1:], lambda s, pt: (l, 0, 0, 0))
    grid_spec = pltpu.PrefetchScalarGridSpec(
        num_scalar_prefetch=1,
        grid=(n_seq,),
        in_specs=[
            pl.BlockSpec(memory_space=pl.ANY),
            pl.BlockSpec((None, NSA_GROUPS, rows, NSA_DH), lambda s, pt: (s, 0, 0, 0)),
            pl.BlockSpec((None, 4 * gw, ls), lambda s, pt: (s, 0, 0)),
            pl.BlockSpec((None, 2 * gw, ls), lambda s, pt: (s, 0, 0)),
            pl.BlockSpec((None, None, 2 * gw, w_len), lambda s, pt: (l, s, 0, 0)),
            pl.BlockSpec((None, NSA_GROUPS, rows, 3), lambda s, pt: (s, 0, 0, 0)),
            lay(w1), lay(b1), lay(w2), lay(b2),
            whole(tabs["cmp"].shape), whole(tabs["sel"].shape), whole(tabs["win"].shape),
            whole(tabs["ovl"].shape), whole(tabs["esel"].shape),
        ],
        out_specs=[
            pl.BlockSpec((None, NSA_GROUPS, rows, NSA_DH), lambda s, pt: (s, 0, 0, 0)),
            pl.BlockSpec((None, 2 * gw, w_len), lambda s, pt: (s, 0, 0)),
        ],
        scratch_shapes=[pltpu.VMEM((2, 4 * gw, t_cols), F32),
                        pltpu.VMEM((2 * gw, w_cols), F32),
                        pltpu.VMEM((2, past, gw), F32),
                        pltpu.SemaphoreType.DMA((2,))],
    )
    return pl.pallas_call(
        kern,
        grid_spec=grid_spec,
        out_shape=[jax.ShapeDtypeStruct((n_seq, NSA_GROUPS, rows, NSA_DH), F32),
                   jax.ShapeDtypeStruct((n_seq, 2 * gw, w_len), F32)],
        compiler_params=pltpu.CompilerParams(dimension_semantics=("arbitrary",),
                                             vmem_limit_bytes=VMEM_LIMIT_BYTES),
        name="nsa_decode",
    )(page_table, cache_t, q_s, kvn_t, winn_t, winbuf_t, gates, w1, b1, w2, b2,
      tabs["cmp"], tabs["sel"], tabs["win"], tabs["ovl"], tabs["esel"])


def _decode_tables(rel_table, *, past, ls, n_cmp, n_sel, w_len):
    rows = NSA_HPG * ls
    t_cols = -(-(n_sel * SEL_LEN) // 128) * 128
    w_cols = -(-(w_len + ls) // 128) * 128
    n_sub = past // CMP_STRIDE
    tok = np.tile(np.arange(ls), NSA_HPG)
    hd = np.repeat(np.arange(NSA_HPG), ls)
    a = np.concatenate([past + tok, past + tok])
    head = np.concatenate([hd, NSA_HPG + hd])
    bc = np.where(np.arange(n_sub) < n_cmp, np.arange(n_sub) * CMP_STRIDE + CMP_LEN - 1, BIG)
    tabs = {
        "cmp": _bias_table(rel_table, a, bc, head, BIG).reshape(NSA_GROUPS, rows, n_sub),
        "sel": _bias_table(rel_table, a, np.arange(t_cols), head, BIG).reshape(NSA_GROUPS, rows, t_cols),
        "win": _bias_table(rel_table, a, past - w_len + np.arange(w_cols), head, WINDOW
                           ).reshape(NSA_GROUPS, rows, w_cols),
    }
    ovl = np.zeros((n_sub, 128), np.float32)
    ovl[:n_cmp, :n_sel] = _block_overlap(n_cmp, n_sel)
    esel = np.zeros((128, t_cols), np.float32)
    esel[np.arange(n_sel * SEL_LEN) // SEL_LEN, np.arange(n_sel * SEL_LEN)] = 1.0
    tabs["ovl"] = jnp.asarray(ovl, BF16)
    tabs["esel"] = jnp.asarray(esel, BF16)
    return tabs


TQ = 128
TKS = 256
SEL_OFFSETS = 3
WIN_OFFSETS = WINDOW // TQ + 2
AUG_ROWS = 32


def _prompt_tables(rel_table, *, ln):
    nq = ln // TQ
    n_sub = ln // CMP_STRIDE
    n_cmp = (ln - CMP_LEN) // CMP_STRIDE + 1
    n_sel = -(-ln // SEL_LEN)
    qq = np.tile(np.arange(TQ), NSA_HPG)
    hd = np.repeat(np.arange(NSA_HPG), TQ)
    rows = NSA_HPG * TQ
    a = np.concatenate([TQ * i + qq - (CMP_LEN - 1) for i in range(nq) for g in range(NSA_GROUPS)])
    head = np.concatenate([NSA_HPG * g + hd for i in range(nq) for g in range(NSA_GROUPS)])
    bc = np.where(np.arange(n_sub) < n_cmp, np.arange(n_sub) * CMP_STRIDE, BIG)
    cmp_t = _bias_table(rel_table, a, bc, head, BIG).reshape(nq, NSA_GROUPS, rows, n_sub)
    far = rel_table[REL_BUCKETS - 1]
    far_hi = far.astype(BF16)
    far_lo = (far - far_hi.astype(F32)).astype(BF16)
    far_eff = far_hi.astype(F32) + far_lo.astype(F32)
    sel_t, win_t, crows = [], [], []
    for g in range(NSA_GROUPS):
        head = NSA_HPG * g + hd
        a = np.concatenate([TQ * w - np.arange(TKS) for w in range(WIN_OFFSETS)])
        sub = jnp.take(far_eff, jnp.asarray(head, jnp.int32))
        sel_t.append(_bias_table_cols(rel_table, a[:SEL_OFFSETS * TKS], -qq, head, sub, BIG
                                      ).reshape(SEL_OFFSETS, TKS, rows))
        win_t.append(_bias_table_cols(rel_table, a, -qq, head, jnp.zeros((rows,), F32), WINDOW
                                      ).reshape(WIN_OFFSETS, TKS, rows))
        cr = jnp.zeros((AUG_ROWS, rows), F32)
        cr = cr.at[0].set(jnp.take(far_hi.astype(F32), jnp.asarray(head, jnp.int32)))
        cr = cr.at[1].set(jnp.take(far_lo.astype(F32), jnp.asarray(head, jnp.int32)))
        crows.append(cr)
    nb = -(-n_sel // 8) * 8
    assert nb <= AUG_ROWS
    ovl_t = np.zeros((nb, n_sub), np.float32)
    ovl_t[:n_sel, :n_cmp] = _block_overlap(n_cmp, n_sel).T
    return {"cmp": cmp_t, "sel": jnp.stack(sel_t), "win": jnp.stack(win_t), "crows": jnp.stack(crows),
            "ovl_t": jnp.asarray(ovl_t, BF16), "n_sel": n_sel}


def _pcompress_kernel(k_ref, v_ref, w1_ref, b1_ref, w2_ref, b2_ref, ck_ref, cv_ref, *, n_sub):
    for k, src, out in ((0, k_ref, ck_ref), (1, v_ref, cv_ref)):
        out[...] = _compress_rows(
            lambda j, src=src: src[pl.ds(j, n_sub, stride=CMP_STRIDE), :],
            n_sub, w1_ref[k], b1_ref[k], w2_ref[k], b2_ref[k])


def _pcompress_call(p, cw, *, l, nb, ln):
    w1, b1, w2, b2 = cw
    n_sub = ln // CMP_STRIDE
    gw = NSA_GROUPS * NSA_DH
    lay = lambda a: pl.BlockSpec((None,) + a.shape[1:], lambda b: (l, 0, 0, 0))
    out = pl.BlockSpec((None, n_sub, gw), lambda b: (b, 0, 0))
    return pl.pallas_call(
        functools.partial(_pcompress_kernel, n_sub=n_sub),
        grid=(nb,),
        in_specs=[pl.BlockSpec((ln, gw), lambda b: (b, P_KV4 // gw)),
                  pl.BlockSpec((ln, gw), lambda b: (b, P_KV4 // gw + 1)),
                  lay(w1), lay(b1), lay(w2), lay(b2)],
        out_specs=[out, out],
        out_shape=[jax.ShapeDtypeStruct((nb, n_sub, gw), F32)] * 2,
        compiler_params=pltpu.CompilerParams(dimension_semantics=("parallel",),
                                             vmem_limit_bytes=VMEM_LIMIT_BYTES),
        name="nsa_compress",
    )(p, p, w1, b1, w2, b2)


def _group_queries(q_tile, g):
    hs = [q_tile[:, (NSA_HPG * g + r) * NSA_DH:(NSA_HPG * g + r + 1) * NSA_DH] for r in range(NSA_HPG)]
    return (jnp.concatenate(hs, axis=0) * NSA_SCALE).astype(BF16)


def _pcmp_kernel(q_ref, ck_ref, cv_ref, bias_ref, wing_ref, ovl_ref, o_ref, mem_ref, *, n_sel, n_top):
    i = pl.program_id(1)
    gw = NSA_GROUPS * NSA_DH
    gates = jax.nn.sigmoid(wing_ref[:, 2 * gw:2 * gw + 3 * NSA_HEADS])
    nbk = ovl_ref.shape[0]
    blk = lax.broadcasted_iota(jnp.int32, (nbk, TQ), 0)
    cur = (i * TQ + lax.broadcasted_iota(jnp.int32, (nbk, TQ), 1)) // SEL_LEN
    forced = (blk == 0) | (blk == cur) | (blk == cur - 1)
    q_tile = q_ref[...]
    outs = []
    for g in range(NSA_GROUPS):
        lo, hi = g * NSA_DH, (g + 1) * NSA_DH
        qg = _group_queries(q_tile, g)
        p = _softmax_rows(_dot_nt(qg, ck_ref[:, lo:hi].astype(BF16)) + bias_ref[g])
        pb = p.astype(BF16)
        o = _dot(pb, cv_ref[:, lo:hi].astype(BF16))
        score_t = _dot_nt(ovl_ref[...], pb[0:TQ])
        for r in range(1, NSA_HPG):
            score_t = score_t + _dot_nt(ovl_ref[...], pb[r * TQ:(r + 1) * TQ])
        sc = jnp.where(forced, jnp.inf, jnp.where(blk <= cur, score_t, -jnp.inf))
        member_t = _topk_member_rows(sc, n_sel, n_top)
        if nbk < AUG_ROWS:
            member_t = jnp.concatenate([member_t, jnp.zeros((AUG_ROWS - nbk, TQ), F32)], axis=0)
        mem_ref[g] = member_t
        for r in range(NSA_HPG):
            h = NSA_HPG * g + r
            outs.append(gates[:, h:h + 1] * o[r * TQ:(r + 1) * TQ])
    o_ref[...] = jnp.concatenate(outs, axis=1)


def _pcmp_call(p, ck, cv, tabs, *, nb, ln):
    nq = ln // TQ
    n_sub = ln // CMP_STRIDE
    gw = NSA_GROUPS * NSA_DH
    rows = NSA_HPG * TQ
    n_sel = tabs["n_sel"]
    cblk = pl.BlockSpec((None, n_sub, gw), lambda b, i: (b, 0, 0))
    return pl.pallas_call(
        functools.partial(_pcmp_kernel, n_sel=n_sel, n_top=min(SEL_TOP, n_sel)),
        grid=(nb, nq),
        in_specs=[pl.BlockSpec((TQ, NSA_WIDTH), lambda b, i: (b * nq + i, P_QA // NSA_WIDTH)),
                  cblk, cblk,
                  pl.BlockSpec((None, NSA_GROUPS, rows, n_sub), lambda b, i: (i, 0, 0, 0)),
                  pl.BlockSpec((TQ, 4 * gw), lambda b, i: (b * nq + i, P_WING // (4 * gw))),
                  pl.BlockSpec(tabs["ovl_t"].shape, lambda b, i: (0, 0))],
        out_specs=[pl.BlockSpec((TQ, NSA_WIDTH), lambda b, i: (b * nq + i, 0)),
                   pl.BlockSpec((None, NSA_GROUPS, AUG_ROWS, TQ), lambda b, i: (b, 0, 0, i))],
        out_shape=[jax.ShapeDtypeStruct((nb * ln, NSA_WIDTH), F32),
                   jax.ShapeDtypeStruct((nb, NSA_GROUPS, AUG_ROWS, ln), F32)],
        compiler_params=pltpu.CompilerParams(dimension_semantics=("parallel", "arbitrary"),
                                             vmem_limit_bytes=VMEM_LIMIT_BYTES),
        name="nsa_cmp_select",
    )(p, ck, cv, tabs["cmp"], p, tabs["ovl_t"])


def _flash_step_t(carry, lg_t, v_aug_t):
    m, acc = carry
    m_new = jnp.maximum(m, jnp.max(lg_t, axis=0, keepdims=True))
    alpha = jnp.exp(m - m_new)
    p_t = jnp.exp(lg_t - m_new).astype(BF16)
    return m_new, alpha * acc + _dot(v_aug_t, p_t)


def _pattn_kernel(q_ref, kv_ref, wkv_ref, wing_ref, mem_ref, ocmp_ref, bsel_ref, bwin_ref, crow_ref, o_ref,
                  ks_scr, vs_scr, kw_scr, vw_scr):
    i = pl.program_id(1)
    gw = NSA_GROUPS * NSA_DH
    cols = NSA_HPG * TQ
    ln = kv_ref.shape[0]

    @pl.when(i == 0)
    def _():
        pos = lax.broadcasted_iota(jnp.int32, (ln, NSA_DH), 0)
        lane = lax.broadcasted_iota(jnp.int32, (ln, NSA_DH), 1)
        extra = ((lane == pos // SEL_LEN) | (lane == AUG_ROWS) | (lane == AUG_ROWS + 1)).astype(F32)
        ones_row = (lax.broadcasted_iota(jnp.int32, (NSA_DH, ln), 0) == 0).astype(F32)
        zeros = jnp.zeros((ln, NSA_DH), F32)
        vs_t = kv_ref[:, 3 * gw:4 * gw].T
        vw_t = wkv_ref[:, gw:2 * gw].T
        for g in range(NSA_GROUPS):
            lo, hi = g * NSA_DH, (g + 1) * NSA_DH
            ks_scr[g] = jnp.concatenate([kv_ref[:, 2 * gw + lo:2 * gw + hi], extra], axis=1).astype(BF16)
            kw_scr[g] = jnp.concatenate([wkv_ref[:, lo:hi], zeros], axis=1).astype(BF16)
            vs_scr[g] = jnp.concatenate([vs_t[lo:hi], ones_row], axis=0).astype(BF16)
            vw_scr[g] = jnp.concatenate([vw_t[lo:hi], ones_row], axis=0).astype(BF16)

    gates_t = jax.nn.sigmoid(wing_ref[:, 2 * gw:3 * gw]).T
    q_t = q_ref[...].T * NSA_SCALE
    m_tile = i // (TKS // TQ)
    near0 = jnp.maximum((i - 1) // (TKS // TQ), 0)
    q_augs = []
    for g in range(NSA_GROUPS):
        mask_t = (mem_ref[g] - 1.0) * (-NEG)
        q_aug = jnp.concatenate(
            [jnp.concatenate([q_t[(NSA_HPG * g + r) * NSA_DH:(NSA_HPG * g + r + 1) * NSA_DH], mask_t], axis=0)
             for r in range(NSA_HPG)], axis=1)
        q_augs.append(jnp.concatenate([q_aug, crow_ref[g]], axis=0).astype(BF16))

    def tile_step(kt, carry, k_scr, v_scr, bias_ref):
        k0 = pl.multiple_of(kt * TKS, TKS)
        out = []
        for g in range(NSA_GROUPS):
            lg_t = _dot(k_scr[g, pl.ds(k0, TKS), :], q_augs[g])
            if bias_ref is not None:
                lg_t = lg_t + bias_ref[g, i - kt * (TKS // TQ)]
            out.append(_flash_step_t(carry[g], lg_t, v_scr[g, :, pl.ds(k0, TKS)]))
        return tuple(out)

    init1 = (jnp.full((1, cols), NEG, F32), jnp.zeros((2 * NSA_DH, cols), F32))
    init = (init1,) * NSA_GROUPS
    carry = lax.fori_loop(0, near0, lambda kt, c: tile_step(kt, c, ks_scr, vs_scr, None), init)
    sel = lax.fori_loop(near0, m_tile + 1, lambda kt, c: tile_step(kt, c, ks_scr, vs_scr, bsel_ref), carry)
    win = lax.fori_loop(jnp.maximum(m_tile - WINDOW // TKS, 0), m_tile + 1,
                        lambda kt, c: tile_step(kt, c, kw_scr, vw_scr, bwin_ref), init)
    pieces = []
    for g in range(NSA_GROUPS):
        acc_s, acc_w = sel[g][1], win[g][1]
        o_s = acc_s[0:NSA_DH] / acc_s[NSA_DH:NSA_DH + 1]
        o_w = acc_w[0:NSA_DH] / acc_w[NSA_DH:NSA_DH + 1]
        for r in range(NSA_HPG):
            h = NSA_HPG * g + r
            cs = slice(r * TQ, (r + 1) * TQ)
            pieces.append(gates_t[NSA_HEADS + h:NSA_HEADS + h + 1] * o_s[:, cs]
                          + gates_t[2 * NSA_HEADS + h:2 * NSA_HEADS + h + 1] * o_w[:, cs])
    o_ref[...] = ocmp_ref[...] + jnp.concatenate(pieces, axis=0).T


def _pattn_call(p, mem, ocmp, tabs, *, nb, ln):
    nq = ln // TQ
    gw = NSA_GROUPS * NSA_DH
    tile = lambda w, cb: pl.BlockSpec((TQ, w), lambda b, i, cb=cb: (b * nq + i, cb))
    seq = lambda cb: pl.BlockSpec((ln, 4 * gw), lambda b, i, cb=cb: (b, cb))
    whole = lambda a: pl.BlockSpec(a.shape, lambda b, i: (0,) * a.ndim)
    k_scr = pltpu.VMEM((NSA_GROUPS, ln, 2 * NSA_DH), BF16)
    v_scr = pltpu.VMEM((NSA_GROUPS, 2 * NSA_DH, ln), BF16)
    return pl.pallas_call(
        _pattn_kernel,
        grid=(nb, nq),
        in_specs=[tile(NSA_WIDTH, P_QA // NSA_WIDTH), seq(P_KV4 // (4 * gw)), seq(P_WING // (4 * gw)),
                  tile(4 * gw, P_WING // (4 * gw)),
                  pl.BlockSpec((None, NSA_GROUPS, AUG_ROWS, TQ), lambda b, i: (b, 0, 0, i)),
                  tile(NSA_WIDTH, 0),
                  whole(tabs["sel"]), whole(tabs["win"]), whole(tabs["crows"])],
        out_specs=tile(NSA_WIDTH, 0),
        out_shape=jax.ShapeDtypeStruct((nb * ln, NSA_WIDTH), F32),
        scratch_shapes=[k_scr, v_scr, k_scr, v_scr],
        compiler_params=pltpu.CompilerParams(dimension_semantics=("parallel", "arbitrary"),
                                             vmem_limit_bytes=VMEM_LIMIT_BYTES),
        name="nsa_sel_win",
    )(p, p, p, p, mem, ocmp, tabs["sel"], tabs["win"], tabs["crows"])


HG_SUB = 16


def _split3(x):
    h = x.astype(BF16)
    r = x - h.astype(F32)
    m = r.astype(BF16)
    return h, m, (r - m.astype(F32)).astype(BF16)


def _row_bcast(x, row, n):
    return jnp.broadcast_to(x[row:row + 1, :], (n, x.shape[1]))


def _hgrn_prompt_kernel(q_ref, f_ref, i_ref, g_ref, lb_ref, gain_ref, tri_ref, ecat_ref, o_ref, s_ref,
                        st_scr, *, n_chunks):
    step = pl.program_id(1)
    c = HG_CHUNK
    n_sub = c // HG_SUB

    @pl.when(step == 0)
    def _():
        st_scr[...] = jnp.zeros_like(st_scr)

    lb = lb_ref[...]
    rowi = lax.broadcasted_iota(jnp.int32, (c, HG_DK), 0)
    rmod = rowi % HG_SUB
    rsub = rowi // HG_SUB
    ti = lax.broadcasted_iota(jnp.int32, (c, c), 0)
    si = lax.broadcasted_iota(jnp.int32, (c, c), 1)
    diag_mask = (ti // HG_SUB == si // HG_SUB)

    for ch in range(n_chunks):
        r0 = ch * c
        fz = f_ref[pl.ds(r0, c), :]
        sg = jax.nn.sigmoid(fz)
        logf = jnp.log(lb + (1.0 - lb) * sg)
        kk = (1.0 - lb) * jax.nn.sigmoid(-fz)
        qq = _silu(q_ref[pl.ds(r0, c), :])
        vv = i_ref[pl.ds(r0, c), :]
        gg = g_ref[pl.ds(r0, c), :]
        h0, h1, h2 = _split3(logf)
        tri = tri_ref[...]
        b_all = _dot(tri, h0) + _dot(tri, h1) + _dot(tri, h2)
        outs = []
        for h in range(HG_HEADS):
            sl = slice(h * HG_DK, (h + 1) * HG_DK)
            b, q, k, v = b_all[:, sl], qq[:, sl], kk[:, sl], vv[:, sl]
            vb = v.astype(BF16)
            st = st_scr[h]
            o = _dot_nt((q * jnp.exp(b)).astype(BF16), st.astype(BF16))
            bstart = jnp.concatenate(
                [jnp.zeros((HG_SUB, HG_DK), F32)]
                + [_row_bcast(b, HG_SUB * i - 1, HG_SUB) for i in range(1, n_sub)], axis=0)
            bend = jnp.concatenate([_row_bcast(b, HG_SUB * (j + 1) - 1, HG_SUB) for j in range(n_sub)], axis=0)
            qh = q * jnp.exp(b - bstart)
            kh = k * jnp.exp(bend - b)
            lhs, rhs = [], []
            for j in range(n_sub - 1):
                bj = b[HG_SUB * (j + 1) - 1:HG_SUB * (j + 1), :]
                dj = jnp.exp(jnp.minimum(bstart - bj, 0.0))
                lhs.append(jnp.where(rsub > j, qh * dj, 0.0).astype(BF16))
                rhs.append(jnp.where(rsub == j, kh, 0.0).astype(BF16))
            a_off = _dot_nt(jnp.concatenate(lhs, axis=1), jnp.concatenate(rhs, axis=1))
            zs = []
            for s in range(HG_SUB):
                ks = jnp.concatenate([_row_bcast(k, HG_SUB * i + s, HG_SUB) for i in range(n_sub)], axis=0)
                bs = jnp.concatenate([_row_bcast(b, HG_SUB * i + s, HG_SUB) for i in range(n_sub)], axis=0)
                dec = jnp.where(rmod >= s, jnp.exp(jnp.minimum(b - bs, 0.0)), 0.0)
                zs.append((q * ks * dec).astype(BF16))
            a_diag = _dot(jnp.concatenate(zs, axis=1), ecat_ref[...])
            a = a_off + jnp.where(diag_mask, a_diag, 0.0)
            o = o + _dot(a.astype(BF16), vb)
            blast = b[c - 1:c, :]
            kdec = (k * jnp.exp(blast - b)).astype(BF16)
            st_scr[h] = st * jnp.exp(blast) + lax.dot_general(
                vb, kdec, (((0,), (0,)), ((), ())), preferred_element_type=F32)
            y = o * lax.rsqrt(jnp.mean(o * o, axis=-1, keepdims=True) + EPS) * gain_ref[...]
            outs.append(y * _silu(gg[:, sl]))
        o_ref[pl.ds(r0, c), :] = jnp.concatenate(outs, axis=1)

    @pl.when(step == pl.num_programs(1) - 1)
    def _():
        for h in range(HG_HEADS):
            s_ref[h] = st_scr[h].T


def _hgrn_prompt_call(p, lb_l, gain_l, *, nb, ln, tr):
    c = HG_CHUNK
    w = HG_HEADS * HG_DK
    steps = ln // tr
    tri = jnp.asarray(np.tril(np.ones((c, c), np.float32)), BF16)
    ecat = np.zeros((HG_SUB * HG_DK, c), np.float32)
    for s in range(HG_SUB):
        ecat[s * HG_DK:(s + 1) * HG_DK, s::HG_SUB] = 1.0
    col = lambda off: pl.BlockSpec((tr, w), lambda b, i, off=off: (b * steps + i, off // w))
    return pl.pallas_call(
        functools.partial(_hgrn_prompt_kernel, n_chunks=tr // c),
        grid=(nb, steps),
        in_specs=[col(P_QB), col(P_FB), col(P_IB), col(P_GB),
                  pl.BlockSpec((1, w), lambda b, i: (0, 0)),
                  pl.BlockSpec((1, HG_DV), lambda b, i: (0, 0)),
                  pl.BlockSpec((c, c), lambda b, i: (0, 0)),
                  pl.BlockSpec((HG_SUB * HG_DK, c), lambda b, i: (0, 0))],
        out_specs=[pl.BlockSpec((tr, w), lambda b, i: (b * steps + i, 0)),
                   pl.BlockSpec((None, HG_HEADS, HG_DK, HG_DV), lambda b, i: (b, 0, 0, 0))],
        out_shape=[jax.ShapeDtypeStruct((nb * ln, w), F32),
                   jax.ShapeDtypeStruct((nb, HG_HEADS, HG_DK, HG_DV), F32)],
        scratch_shapes=[pltpu.VMEM((HG_HEADS, HG_DV, HG_DK), F32)],
        compiler_params=pltpu.CompilerParams(dimension_semantics=("parallel", "arbitrary"),
                                             vmem_limit_bytes=VMEM_LIMIT_BYTES),
        name="hgrn_prompt",
    )(p, p, p, p, lb_l.reshape(1, w), gain_l.reshape(1, HG_DV), tri, jnp.asarray(ecat, BF16))


HG_SEQ_BLOCK = 4


def _hgrn_decode_kernel(q_ref, f_ref, i_ref, g_ref, lb_ref, gain_ref, s0_ref, o_ref, s_ref, *, ls):
    lb = lb_ref[...]
    rowi = lax.broadcasted_iota(jnp.int32, (8, HG_DK), 0)
    zpad = jnp.zeros((8 - ls, HG_DK), F32)
    for n in range(q_ref.shape[0]):
        fz = f_ref[n]
        logf = jnp.log(lb + (1.0 - lb) * jax.nn.sigmoid(fz))
        kk = (1.0 - lb) * jax.nn.sigmoid(-fz)
        qq = _silu(q_ref[n])
        vv = i_ref[n]
        gg = g_ref[n]
        brows = [logf[0:1]]
        for t in range(1, ls):
            brows.append(brows[-1] + logf[t:t + 1])
        b_all = jnp.concatenate(brows, axis=0)
        outs = []
        for h in range(HG_HEADS):
            sl = slice(h * HG_DK, (h + 1) * HG_DK)
            b, q, k, v = b_all[:, sl], qq[:, sl], kk[:, sl], vv[:, sl]
            st = s0_ref[n, h]
            blast = b[ls - 1:ls]
            e_last = jnp.exp(blast)
            e_hi = e_last.astype(BF16)
            e_lo = (e_last - e_hi.astype(F32)).astype(BF16)
            lhs = jnp.concatenate([(k * jnp.exp(blast - b)).astype(BF16), e_hi, e_lo,
                                   jnp.zeros((8 - ls - 2, HG_DK), BF16)], axis=0)
            ones2 = ((rowi >= ls) & (rowi < ls + 2)).astype(F32)
            rhs = jnp.concatenate([jnp.concatenate([v, zpad], axis=0), ones2], axis=1).astype(BF16)
            upd = lax.dot_general(lhs, rhs, (((0,), (0,)), ((), ())), preferred_element_type=F32)
            s_ref[n, h] = upd[:, HG_DV:] * st + upd[:, :HG_DV]
            o = _dot(jnp.concatenate([q * jnp.exp(b), zpad], axis=0).astype(BF16), st.astype(BF16))[0:ls]
            for s in range(ls):
                keep = rowi[0:ls] >= s
                dec = jnp.where(keep, jnp.exp(jnp.minimum(b - b[s:s + 1], 0.0)), 0.0)
                a_s = jnp.sum(q * k[s:s + 1] * dec, axis=-1, keepdims=True)
                o = o + a_s * v[s:s + 1]
            y = o * lax.rsqrt(jnp.mean(o * o, axis=-1, keepdims=True) + EPS) * gain_ref[...]
            outs.append(y * _silu(gg[:, sl]))
        o_ref[n] = jnp.concatenate(outs, axis=1)


def _hgrn_decode_call(qb, fb, ib, gb, lb_l, gain_l, state, *, l):
    n_seq, ls, w = qb.shape
    sb = HG_SEQ_BLOCK
    assert ls + 2 <= 8 and n_seq % sb == 0
    tok = pl.BlockSpec((sb, ls, w), lambda s: (s, 0, 0))
    return pl.pallas_call(
        functools.partial(_hgrn_decode_kernel, ls=ls),
        grid=(n_seq // sb,),
        in_specs=[tok, tok, tok, tok,
                  pl.BlockSpec((1, w), lambda s: (0, 0)),
                  pl.BlockSpec((1, HG_DV), lambda s: (0, 0)),
                  pl.BlockSpec((None, sb, HG_HEADS, HG_DK, HG_DV), lambda s: (l, s, 0, 0, 0))],
        out_specs=[tok, pl.BlockSpec((sb, HG_HEADS, HG_DK, HG_DV), lambda s: (s, 0, 0, 0))],
        out_shape=[jax.ShapeDtypeStruct((n_seq, ls, w), F32),
                   jax.ShapeDtypeStruct((n_seq, HG_HEADS, HG_DK, HG_DV), F32)],
        compiler_params=pltpu.CompilerParams(dimension_semantics=("parallel",)),
        name="hgrn_decode",
    )(qb, fb, ib, gb, lb_l.reshape(1, w), gain_l.reshape(1, HG_DV), state)


def kernel(x_prompt, x_sample, cache_nsa_kv, state_win_kv, state_hgrn, page_table, c_prompt, c_sample,
           norm_g, ada_w, ada_b, ffn_w_in, ffn_w_out, w_in, cmp_w1, cmp_b1, cmp_w2, cmp_b2, rel_table,
           hgrn_lb_logits, hgrn_norm_g, w_branch_nsa, w_branch_hgrn, w_out, final_g):
    depth = norm_g.shape[0]
    bp, lp, d = x_prompt.shape
    bs, ls, _ = x_sample.shape
    n_seq, n_pages = page_table.shape
    n_pool, page = cache_nsa_kv.shape[1:3]
    past = n_pages * page
    w_len = state_win_kv.shape[2]
    gw = NSA_GROUPS * NSA_DH
    assert bs == 128 and n_seq == bs, "sample rows are tiled per 128 sequences"
    assert lp % TKS == 0 and lp >= WINDOW and (lp // CMP_STRIDE) % 128 == 0
    n_cmp_s = (past + ls - CMP_LEN) // CMP_STRIDE + 1
    n_sel_s = -(-(past + ls) // SEL_LEN)
    assert (n_cmp_s - 1) * CMP_STRIDE + CMP_LEN <= past and (past // CMP_STRIDE) % 128 == 0
    assert past % SEL_LEN == 0 and ls <= 8 and past // SEL_LEN == (past + ls - 1) // SEL_LEN

    lb = jnp.cumsum(jax.nn.softmax(hgrn_lb_logits.astype(F32), axis=0), axis=0)
    lb = lb - lb[:1]

    w_in_p = _permute_w_in(w_in)
    ffn_w_in_b = ffn_w_in.astype(BF16)
    ffn_w_out_b = ffn_w_out.astype(BF16)
    wpa_b = w_branch_nsa.astype(BF16)
    wpb_b = w_branch_hgrn.astype(BF16)
    wo_b = w_out.astype(BF16)
    cw = _compress_weights(cmp_w1, cmp_b1, cmp_w2, cmp_b2)
    tabs_p = _prompt_tables(rel_table, ln=lp)
    tabs_s = _decode_tables(rel_table, past=past, ls=ls, n_cmp=n_cmp_s, n_sel=n_sel_s, w_len=w_len)
    cache_t = cache_nsa_kv.reshape(depth, n_pool, page, 4 * gw).transpose(0, 1, 3, 2)
    winbuf_t = state_win_kv.reshape(depth, bs, w_len, 2 * gw).transpose(0, 1, 3, 2)

    mods = _mods_call(jnp.concatenate([c_sample, c_prompt], axis=0), ada_w, ada_b)

    def sublayers(x2, l, tm, seq_tiles, mixers):
        x2 = _ffn_call(x2, norm_g, mods, ffn_w_in_b, ffn_w_out_b, final_g, l=l, s=0, sub=0, tm=tm, tf=256,
                       seq_tiles=seq_tiles, final=False)
        p = _proj_call(x2, norm_g, mods, w_in_p, l=l, tm=tm, tn=512, seq_tiles=seq_tiles)
        o_a, o_b, extras = mixers(p)
        x2 = _merge_call(x2, o_a, o_b, p, mods, wpa_b, wpb_b, wo_b, l=l, tm=tm, seq_tiles=seq_tiles)
        x2 = _ffn_call(x2, norm_g, mods, ffn_w_in_b, ffn_w_out_b, final_g, l=l, s=1, sub=2, tm=tm, tf=256,
                       seq_tiles=seq_tiles, final=(l == depth - 1))
        return x2, extras

    tm_p = 1024 if lp % 1024 == 0 else lp
    x2 = x_prompt.reshape(bp * lp, d)
    kv_p, win_p, hg_p = [], [], []
    for l in range(depth):
        def prompt_mixers(p, l=l):
            ck, cv = _pcompress_call(p, cw, l=l, nb=bp, ln=lp)
            ocmp, mem = _pcmp_call(p, ck, cv, tabs_p, nb=bp, ln=lp)
            o_a = _pattn_call(p, mem, ocmp, tabs_p, nb=bp, ln=lp)
            o_b, s_new = _hgrn_prompt_call(p, lb[l], hgrn_norm_g[l], nb=bp, ln=lp, tr=128)
            p3 = p.reshape(bp, lp, P_TOTAL)
            kv4 = p3[:, :, P_KV4:P_KV4 + 4 * gw].reshape(bp, lp, 4, NSA_GROUPS, NSA_DH)
            nwin = p3[:, lp - min(WINDOW, lp):, P_WING:P_WING + 2 * gw].reshape(bp, -1, 2, NSA_GROUPS, NSA_DH)
            return o_a, o_b, (kv4, nwin, s_new)

        x2, (kv4, nwin, s_new) = sublayers(x2, l, tm_p, (bs, lp // tm_p), prompt_mixers)
        kv_p.append(kv4)
        win_p.append(nwin)
        hg_p.append(s_new)
    y_p = x2.reshape(bp, lp, d)

    x2 = x_sample.transpose(1, 0, 2).reshape(ls * bs, d)
    kv_s, win_s, hg_s = [], [], []
    for l in range(depth):
        def sample_mixers(p, l=l):
            ps = p.reshape(ls, bs, P_TOTAL).transpose(1, 0, 2)
            q_s = ps[:, :, P_QA:P_QA + NSA_WIDTH].reshape(bs, ls, NSA_GROUPS, NSA_HPG, NSA_DH)
            q_s = q_s.transpose(0, 2, 3, 1, 4).reshape(bs, NSA_GROUPS, NSA_HPG * ls, NSA_DH)
            ga = ps[:, :, P_WING + 2 * gw:P_WING + 2 * gw + 3 * NSA_HEADS].reshape(bs, ls, 3, NSA_GROUPS, NSA_HPG)
            ga = ga.transpose(0, 3, 4, 1, 2).reshape(bs, NSA_GROUPS, NSA_HPG * ls, 3)
            kvn = ps[:, :, P_KV4:P_KV4 + 4 * gw]
            winn = ps[:, :, P_WING:P_WING + 2 * gw]
            o, nwin_t = _decode_call(cache_t, page_table, q_s, kvn.transpose(0, 2, 1), winn.transpose(0, 2, 1),
                                     winbuf_t, ga, cw, tabs_s, l=l, past=past, n_sel=n_sel_s)
            nwin = nwin_t.transpose(0, 2, 1)
            o_a = o.reshape(bs, NSA_GROUPS, NSA_HPG, ls, NSA_DH).transpose(3, 0, 1, 2, 4).reshape(ls * bs, NSA_WIDTH)
            o_b, s_new = _hgrn_decode_call(ps[:, :, P_QB:P_QB + 512], ps[:, :, P_FB:P_FB + 512],
                                           ps[:, :, P_IB:P_IB + 512], ps[:, :, P_GB:P_GB + 512],
                                           lb[l], hgrn_norm_g[l], state_hgrn, l=l)
            o_b = o_b.transpose(1, 0, 2).reshape(ls * bs, HG_WIDTH)
            return o_a, o_b, (kvn.reshape(bs, ls, 4, NSA_GROUPS, NSA_DH),
                              nwin.reshape(bs, w_len, 2, NSA_GROUPS, NSA_DH), s_new)

        x2, (kv4, nwin, s_new) = sublayers(x2, l, ls * bs, None, sample_mixers)
        kv_s.append(kv4)
        win_s.append(nwin)
        hg_s.append(s_new)
    y_s = x2.reshape(ls, bs, d).transpose(1, 0, 2)

    return (y_p, y_s, jnp.stack(kv_p), jnp.stack(kv_s), jnp.stack(win_p), jnp.stack(win_s),
            jnp.stack(hg_p), jnp.stack(hg_s))
```

```python
import functools
import math

import jax
import jax.numpy as jnp
import numpy as np
from jax import lax
from jax.experimental import pallas as pl
from jax.experimental.pallas import tpu as pltpu

NSA_HEADS = 8
NSA_GROUPS = 2
NSA_HPG = NSA_HEADS // NSA_GROUPS
NSA_DH = 64
NSA_WIDTH = NSA_HEADS * NSA_DH
NSA_SCALE = NSA_DH ** -0.5
CMP_LEN = 32
CMP_STRIDE = 16
CMP_RATIO = CMP_LEN // CMP_STRIDE
CMP_HID = 128
SEL_LEN = 64
SEL_TOP = 16
WINDOW = 512
HG_HEADS = 4
HG_DK = 128
HG_DV = 128
HG_WIDTH = HG_HEADS * HG_DV
HG_CHUNK = 64
REL_BUCKETS = 32
REL_MAX_DIST = 128
EPS = 1e-6

BF16 = jnp.bfloat16
F32 = jnp.float32

VMEM_LIMIT_BYTES = 56 * 1024 * 1024

P_KV4, P_WING, P_QB, P_FB, P_IB, P_GB, P_TOTAL = (0, 512, 1024, 1536, 2048, 2560, 3072)


def _split_w_in(w_in):
    d_model = w_in.shape[1]
    sizes = (NSA_WIDTH, 6 * NSA_GROUPS * NSA_DH, 3 * NSA_HEADS, HG_HEADS * HG_DK, HG_HEADS * HG_DK,
             HG_WIDTH, HG_WIDTH, d_model, d_model)
    offs = np.concatenate([[0], np.cumsum(sizes)])
    qa, kva, ga, qb, fb, ib, gb, ma, mb = [w_in[:, :, int(offs[i]):int(offs[i + 1])] for i in range(9)]
    pad = jnp.zeros(w_in.shape[:2] + (P_QB - P_KV4 - kva.shape[2] - ga.shape[2],), w_in.dtype)
    out = jnp.concatenate([kva, ga, pad, qb, fb, ib, gb], axis=2)
    assert out.shape[2] == P_TOTAL
    return (out.astype(BF16), qa.transpose(0, 2, 1).astype(BF16),
            jnp.concatenate([ma, mb], axis=2).astype(BF16))


def _mods_kernel(c_ref, w_ref, b_ref, o_ref):
    c = c_ref[...]
    sc = (c * jax.nn.sigmoid(c)).astype(BF16)
    o_ref[...] = jnp.dot(sc, w_ref[...].astype(BF16), preferred_element_type=F32) + b_ref[...]


def _mods_call(c_all, ada_w, ada_b):
    depth, d, nd = ada_w.shape
    n = c_all.shape[0]
    b3 = ada_b.reshape(depth, 1, nd)
    return pl.pallas_call(
        _mods_kernel,
        grid=(depth, nd // d),
        in_specs=[
            pl.BlockSpec((n, d), lambda l, j: (0, 0)),
            pl.BlockSpec((None, d, d), lambda l, j: (l, 0, j)),
            pl.BlockSpec((None, 1, d), lambda l, j: (l, 0, j)),
        ],
        out_specs=pl.BlockSpec((None, None, n, d), lambda l, j: (l, j, 0, 0)),
        out_shape=jax.ShapeDtypeStruct((depth, nd // d, n, d), F32),
        compiler_params=pltpu.CompilerParams(dimension_semantics=("arbitrary", "arbitrary")),
        name="adaln_mods",
    )(c_all, ada_w, b3)


def _rows(m, tm):
    r = m.shape[0]
    if r == 1 or r == tm:
        return m
    return jnp.concatenate([m] * (tm // r), axis=0)


def _norm_mod(x, g, shift, scale):
    y = x * lax.rsqrt(jnp.mean(x * x, axis=-1, keepdims=True) + EPS) * g
    return y * (1.0 + scale) + shift


def _mod_spec(mods, l, k, tm, seq_tiles):
    if seq_tiles is not None:
        row0, tiles_per_seq = seq_tiles
        d = mods.shape[-1]
        m5 = mods.reshape(mods.shape[0], mods.shape[1], mods.shape[2], 1, d)
        return m5, pl.BlockSpec((None, None, None, 1, d),
                                lambda i, j, l=l, k=k: (l, k, row0 + i // tiles_per_seq, 0, 0))
    d = mods.shape[-1]
    return mods, pl.BlockSpec((None, None, 128, d), lambda i, j, l=l, k=k: (l, k, 0, 0))


def _ffn_kernel(x_ref, g_ref, sh_ref, sc_ref, gt_ref, wa_ref, wb_ref, wo_ref, fg_ref, o_ref,
                n_scr, acc_scr, *, nf, final):
    j = pl.program_id(1)
    tm = x_ref.shape[0]

    @pl.when(j == 0)
    def _():
        n = _norm_mod(x_ref[...], g_ref[...], _rows(sh_ref[...], tm), _rows(sc_ref[...], tm))
        n_scr[...] = n.astype(BF16)
        acc_scr[...] = jnp.zeros_like(acc_scr)

    n = n_scr[...]
    a = jnp.dot(n, wa_ref[...].astype(BF16), preferred_element_type=F32)
    b = jnp.dot(n, wb_ref[...].astype(BF16), preferred_element_type=F32)
    h = (a * jax.nn.sigmoid(a) * b).astype(BF16)
    acc_scr[...] += jnp.dot(h, wo_ref[...].astype(BF16), preferred_element_type=F32)

    @pl.when(j == nf - 1)
    def _():
        y = x_ref[...] + 0.5 * _rows(gt_ref[...], tm) * acc_scr[...]
        if final:
            y = y * lax.rsqrt(jnp.mean(y * y, axis=-1, keepdims=True) + EPS) * fg_ref[...]
        o_ref[...] = y


def _ffn_call(x, norm_g, mods, ffn_w_in, ffn_w_out, final_g, *, l, s, sub, tm, tf, seq_tiles, final):
    t, d = x.shape
    d_ff = ffn_w_out.shape[2]
    nf = d_ff // tf
    g2 = norm_g[l, sub].reshape(1, d)
    fg2 = final_g.reshape(1, d)
    m_sh, sp_sh = _mod_spec(mods, l, 3 * sub, tm, seq_tiles)
    m_sc, sp_sc = _mod_spec(mods, l, 3 * sub + 1, tm, seq_tiles)
    m_gt, sp_gt = _mod_spec(mods, l, 3 * sub + 2, tm, seq_tiles)
    return pl.pallas_call(
        functools.partial(_ffn_kernel, nf=nf, final=final),
        grid=(t // tm, nf),
        in_specs=[
            pl.BlockSpec((tm, d), lambda i, j: (i, 0)),
            pl.BlockSpec((1, d), lambda i, j: (0, 0)),
            sp_sh, sp_sc, sp_gt,
            pl.BlockSpec((None, None, d, tf), lambda i, j: (l, s, 0, j)),
            pl.BlockSpec((None, None, d, tf), lambda i, j: (l, s, 0, nf + j)),
            pl.BlockSpec((None, None, tf, d), lambda i, j: (l, s, j, 0)),
            pl.BlockSpec((1, d), lambda i, j: (0, 0)),
        ],
        out_specs=pl.BlockSpec((tm, d), lambda i, j: (i, 0)),
        out_shape=jax.ShapeDtypeStruct((t, d), F32),
        scratch_shapes=[pltpu.VMEM((tm, d), BF16), pltpu.VMEM((tm, d), F32)],
        compiler_params=pltpu.CompilerParams(
            dimension_semantics=("parallel", "arbitrary"), vmem_limit_bytes=VMEM_LIMIT_BYTES),
        name="ffn",
    )(x, g2, m_sh, m_sc, m_gt, ffn_w_in, ffn_w_in, ffn_w_out, fg2)


def _proj_kernel(x_ref, g_ref, sh_ref, sc_ref, w_ref, wq_ref, o_ref, qt_ref, n_scr):
    j = pl.program_id(1)
    tm = x_ref.shape[0]

    @pl.when(j == 0)
    def _():
        n = _norm_mod(x_ref[...], g_ref[...], _rows(sh_ref[...], tm), _rows(sc_ref[...], tm))
        n_scr[...] = n.astype(BF16)
        qt_ref[...] = _dot_nt(wq_ref[...], n_scr[...]).astype(BF16)

    o_ref[...] = jnp.dot(n_scr[...], w_ref[...], preferred_element_type=F32)


def _proj_call(x, norm_g, mods, w_in_p, wq_t, *, l, tm, tn, seq_tiles):
    t, d = x.shape
    n_out = w_in_p.shape[2]
    g2 = norm_g[l, 1].reshape(1, d)
    m_sh, sp_sh = _mod_spec(mods, l, 3, tm, seq_tiles)
    m_sc, sp_sc = _mod_spec(mods, l, 4, tm, seq_tiles)
    return pl.pallas_call(
        _proj_kernel,
        grid=(t // tm, n_out // tn),
        in_specs=[
            pl.BlockSpec((tm, d), lambda i, j: (i, 0)),
            pl.BlockSpec((1, d), lambda i, j: (0, 0)),
            sp_sh, sp_sc,
            pl.BlockSpec((None, d, tn), lambda i, j: (l, 0, j)),
            pl.BlockSpec((None, NSA_WIDTH, d), lambda i, j: (l, 0, 0)),
        ],
        out_specs=[pl.BlockSpec((tm, tn), lambda i, j: (i, j)),
                   pl.BlockSpec((None, NSA_WIDTH, tm), lambda i, j: (i, 0, 0))],
        out_shape=[jax.ShapeDtypeStruct((t, n_out), F32),
                   jax.ShapeDtypeStruct((t // tm, NSA_WIDTH, tm), BF16)],
        scratch_shapes=[pltpu.VMEM((tm, d), BF16)],
        compiler_params=pltpu.CompilerParams(
            dimension_semantics=("parallel", "arbitrary"), vmem_limit_bytes=VMEM_LIMIT_BYTES),
        name="in_proj",
    )(x, g2, m_sh, m_sc, w_in_p, wq_t)


def _merge_kernel(x_ref, oa_ref, ob_ref, g_ref, sh_ref, sc_ref, gt_ref, wm_ref, wpa_ref, wpb_ref, wo_ref, o_ref):
    tm, d = x_ref.shape
    x = x_ref[...]
    n = _norm_mod(x, g_ref[...], _rows(sh_ref[...], tm), _rows(sc_ref[...], tm)).astype(BF16)
    ma = jnp.dot(n, wm_ref[:, 0:d], preferred_element_type=F32)
    mb = jnp.dot(n, wm_ref[:, d:2 * d], preferred_element_type=F32)
    ya = jnp.dot(oa_ref[...].astype(BF16), wpa_ref[...], preferred_element_type=F32)
    yb = jnp.dot(ob_ref[...].astype(BF16), wpb_ref[...], preferred_element_type=F32)
    merged = jax.nn.sigmoid(ma) * ya + jax.nn.sigmoid(mb) * yb
    y = jnp.dot(merged.astype(BF16), wo_ref[...], preferred_element_type=F32)
    o_ref[...] = x + _rows(gt_ref[...], tm) * y


def _merge_call(x, o_a, o_b, norm_g, mods, w_gate, wpa, wpb, wo, *, l, tm, seq_tiles):
    t, d = x.shape
    g2 = norm_g[l, 1].reshape(1, d)
    m_sh, sp_sh = _mod_spec(mods, l, 3, tm, seq_tiles)
    m_sc, sp_sc = _mod_spec(mods, l, 4, tm, seq_tiles)
    m_gt, sp_gt = _mod_spec(mods, l, 5, tm, seq_tiles)
    one = lambda i, j: (i, 0)
    return pl.pallas_call(
        _merge_kernel,
        grid=(t // tm, 1),
        in_specs=[
            pl.BlockSpec((tm, d), one),
            pl.BlockSpec((tm, NSA_WIDTH), one),
            pl.BlockSpec((tm, HG_WIDTH), one),
            pl.BlockSpec((1, d), lambda i, j: (0, 0)),
            sp_sh, sp_sc, sp_gt,
            pl.BlockSpec((None, d, 2 * d), lambda i, j: (l, 0, 0)),
            pl.BlockSpec((None, NSA_WIDTH, d), lambda i, j: (l, 0, 0)),
            pl.BlockSpec((None, HG_WIDTH, d), lambda i, j: (l, 0, 0)),
            pl.BlockSpec((None, d, d), lambda i, j: (l, 0, 0)),
        ],
        out_specs=pl.BlockSpec((tm, d), one),
        out_shape=jax.ShapeDtypeStruct((t, d), F32),
        compiler_params=pltpu.CompilerParams(
            dimension_semantics=("parallel", "arbitrary"), vmem_limit_bytes=VMEM_LIMIT_BYTES),
        name="merge_out",
    )(x, o_a, o_b, g2, m_sh, m_sc, m_gt, w_gate, wpa, wpb, wo)


NEG = -1e30


def _bucket_thresholds():
    n = np.arange(0, 4 * REL_MAX_DIST)
    exact = REL_BUCKETS // 2
    large = exact + np.floor(np.log(np.maximum(n, 1) / exact) / math.log(REL_MAX_DIST / exact)
                             * (REL_BUCKETS - exact)).astype(np.int64)
    bucket = np.where(n < exact, n, np.minimum(large, REL_BUCKETS - 1))
    return [int(np.min(n[bucket >= k])) for k in range(REL_BUCKETS)]


_THRESHOLDS = _bucket_thresholds()


def _bias_kernel(tab_ref, a_ref, b_ref, o_ref, *, hi):
    dist = a_ref[...] - b_ref[...]
    tab = tab_ref[...]
    v = jnp.broadcast_to(tab[:, REL_BUCKETS - 1:REL_BUCKETS], dist.shape)
    for k in range(REL_BUCKETS - 2, -1, -1):
        v = jnp.where(dist < _THRESHOLDS[k + 1], tab[:, k:k + 1], v)
    o_ref[...] = jnp.where((dist >= 0) & (dist < hi), v, NEG)


def _bias_table(rel_table, a, b, head, hi):
    r, c = a.shape[0], b.shape[0]
    tr = min(r, 512)
    assert r % tr == 0
    tab = jnp.take(rel_table, jnp.asarray(head, jnp.int32), axis=1).T
    return pl.pallas_call(
        functools.partial(_bias_kernel, hi=hi),
        grid=(r // tr,),
        in_specs=[pl.BlockSpec((tr, REL_BUCKETS), lambda i: (i, 0)),
                  pl.BlockSpec((tr, 1), lambda i: (i, 0)),
                  pl.BlockSpec((1, c), lambda i: (0, 0))],
        out_specs=pl.BlockSpec((tr, c), lambda i: (i, 0)),
        out_shape=jax.ShapeDtypeStruct((r, c), F32),
        name="rel_bias_table",
    )(tab, jnp.asarray(a, jnp.int32).reshape(r, 1), jnp.asarray(b, jnp.int32).reshape(1, c))


def _bias_cols_kernel(tab_ref, a_ref, b_ref, sub_ref, o_ref, *, hi):
    dist = a_ref[...] - b_ref[...]
    tab = tab_ref[...]
    v = jnp.broadcast_to(tab[REL_BUCKETS - 1:REL_BUCKETS, :], dist.shape)
    for k in range(REL_BUCKETS - 2, -1, -1):
        v = jnp.where(dist < _THRESHOLDS[k + 1], tab[k:k + 1, :], v)
    o_ref[...] = jnp.where((dist >= 0) & (dist < hi), v - sub_ref[...], NEG)


def _bias_table_cols(rel_table, a, b, head, sub, hi):
    r, c = a.shape[0], b.shape[0]
    tr = min(r, 256)
    assert r % tr == 0
    tab = jnp.take(rel_table, jnp.asarray(head, jnp.int32), axis=1)
    return pl.pallas_call(
        functools.partial(_bias_cols_kernel, hi=hi),
        grid=(r // tr,),
        in_specs=[pl.BlockSpec((REL_BUCKETS, c), lambda i: (0, 0)),
                  pl.BlockSpec((tr, 1), lambda i: (i, 0)),
                  pl.BlockSpec((1, c), lambda i: (0, 0)),
                  pl.BlockSpec((1, c), lambda i: (0, 0))],
        out_specs=pl.BlockSpec((tr, c), lambda i: (i, 0)),
        out_shape=jax.ShapeDtypeStruct((r, c), F32),
        name="rel_bias_table_t",
    )(tab, jnp.asarray(a, jnp.int32).reshape(r, 1), jnp.asarray(b, jnp.int32).reshape(1, c), sub.reshape(1, c))


BIG = 1 << 30


def _dot_nt(a, b):
    return lax.dot_general(a, b, (((1,), (1,)), ((), ())), preferred_element_type=F32)


def _dot(a, b):
    return jnp.dot(a, b, preferred_element_type=F32)


def _silu(x):
    return x * jax.nn.sigmoid(x)


def _block_overlap(n_cmp, n_sel):
    c0 = np.arange(n_cmp) * CMP_STRIDE
    s0 = np.arange(n_sel) * SEL_LEN
    return ((c0[:, None] <= s0[None] + SEL_LEN - 1) & (c0[:, None] + CMP_LEN - 1 >= s0[None])).astype(np.float32)


def _compress_weights(cmp_w1, cmp_b1, cmp_w2, cmp_b2):
    eye = jnp.eye(NSA_GROUPS, dtype=F32)
    w1 = jnp.einsum('lkrjde,gh->lkjgdrhe', cmp_w1, eye)
    depth = cmp_w1.shape[0]
    w1 = w1.reshape(depth, 2, CMP_STRIDE * NSA_GROUPS * NSA_DH, CMP_RATIO * NSA_GROUPS * CMP_HID).astype(BF16)
    w2 = jnp.einsum('lked,gh->lkgehd', cmp_w2, eye).reshape(depth, 2, NSA_GROUPS * CMP_HID, NSA_GROUPS * NSA_DH)
    b1 = jnp.tile(cmp_b1, (1, 1, NSA_GROUPS)).reshape(depth, 2, 1, NSA_GROUPS * CMP_HID)
    b2 = jnp.tile(cmp_b2, (1, 1, NSA_GROUPS)).reshape(depth, 2, 1, NSA_GROUPS * NSA_DH)
    return w1, b1, w2.astype(BF16), b2


def _compress_rows(load_piece, n_sub, w1, b1, w2, b2):
    x = jnp.concatenate([load_piece(j).astype(BF16) for j in range(CMP_STRIDE)], axis=1)
    hr = _dot(x, w1)
    half = NSA_GROUPS * CMP_HID
    h = hr[:, :half] + pltpu.roll(hr[:, half:], n_sub - 1, 0) + b1
    return _dot(_silu(h).astype(BF16), w2) + b2


def _topk_member_rows(sc, n_sel, n_top):
    blk = lax.broadcasted_iota(jnp.int32, sc.shape, 0)
    cnt = jnp.zeros(sc.shape, jnp.int32)
    for i in range(n_sel):
        si = sc[i:i + 1, :]
        ahead = (si > sc) | ((si == sc) & (blk > i))
        cnt = cnt + ahead.astype(jnp.int32)
    return ((cnt < n_top) & (blk < n_sel)).astype(F32)


def _topk_member_lanes(sc, n_sel, n_top):
    blk = lax.broadcasted_iota(jnp.int32, sc.shape, 1)
    cnt = jnp.zeros(sc.shape, jnp.int32)
    for i in range(n_sel):
        si = sc[:, i:i + 1]
        ahead = (si > sc) | ((si == sc) & (blk > i))
        cnt = cnt + ahead.astype(jnp.int32)
    return ((cnt < n_top) & (blk < n_sel)).astype(F32)


def _softmax_rows(lg):
    m = jnp.max(lg, axis=-1, keepdims=True)
    e = jnp.exp(lg - m)
    s = jnp.sum(e, axis=-1, keepdims=True)
    return jnp.where(m > 0.5 * NEG, e / s, 0.0)


DEC_SEQ_BLOCK = 2


def _decode_kernel(pt_ref, cache_ref, q_ref, kvn_ref, winn_ref, winbuf_ref, gate_ref,
                   w1_ref, b1_ref, w2_ref, b2_ref, bcmp_ref, bsel_ref, bwin_ref, ovl_ref, esel_ref, _,
                   o_ref, newwin_ref, buf, wbuf, cbuf, sem, *, l, n_pages, page, ls, past, n_sel, n_top):
    s = pl.program_id(0)
    n_steps = pl.num_programs(0)
    slot = s % 2
    nsq = DEC_SEQ_BLOCK
    t_cols = buf.shape[3]
    w_cols = wbuf.shape[2]
    w_len = winbuf_ref.shape[2]
    gw = NSA_GROUPS * NSA_DH

    def page_copy(step, sl, j, pg):
        return pltpu.make_async_copy(cache_ref.at[l, pt_ref[step * nsq + j, pg]],
                                     buf.at[sl, j, :, pl.ds(pg * page, page)], sem.at[sl])

    def start_all(step, sl):
        for j in range(nsq):
            for pg in range(n_pages):
                page_copy(step, sl, j, pg).start()

    @pl.when(s == 0)
    def _():
        start_all(0, 0)
        for sl in range(2):
            for j in range(nsq):
                buf[sl, j, :, pl.ds(past, t_cols - past)] = jnp.zeros((buf.shape[2], t_cols - past), F32)
        for j in range(nsq):
            wbuf[j, :, pl.ds(w_len, w_cols - w_len)] = jnp.zeros((wbuf.shape[1], w_cols - w_len), F32)

    @pl.when(s + 1 < n_steps)
    def _():
        start_all(s + 1, 1 - slot)

    for j in range(nsq):
        for pg in range(n_pages):
            page_copy(s, slot, j, pg).wait()

    n_sub = past // CMP_STRIDE
    half = NSA_GROUPS * CMP_HID
    units = [(j, g) for j in range(nsq) for g in range(NSA_GROUPS)]
    xb = [buf.at[slot, j] for j in range(nsq)]
    for j in range(nsq):
        xb[j][:, pl.ds(past, ls)] = kvn_ref[j]
        wbuf[j, :, pl.ds(0, w_len)] = winbuf_ref[j]
        wbuf[j, :, pl.ds(w_len, ls)] = winn_ref[j]
        newwin_ref[j] = pltpu.roll(wbuf[j], w_cols - ls, 1)[:, 0:w_len]
        for k in range(2):
            cbuf[j, k] = xb[j][k * gw:(k + 1) * gw, pl.ds(0, past)].T

    jk = [(j, k) for j in range(nsq) for k in range(2)]
    xs = [jnp.concatenate([cbuf[j, k, pl.ds(t, n_sub, stride=CMP_STRIDE), :].astype(BF16)
                           for t in range(CMP_STRIDE)], axis=1) for j, k in jk]
    hrs = [_dot(x, w1_ref[k]) for x, (j, k) in zip(xs, jk)]
    hs = [hr[:, :half] + pltpu.roll(hr[:, half:], n_sub - 1, 0) + b1_ref[k] for hr, (j, k) in zip(hrs, jk)]
    summ = [_dot(_silu(h).astype(BF16), w2_ref[k]) + b2_ref[k] for h, (j, k) in zip(hs, jk)]
    ck = {j: summ[2 * j] for j in range(nsq)}
    cv = {j: summ[2 * j + 1] for j in range(nsq)}

    tokpos = past + lax.broadcasted_iota(jnp.int32, (ls, 128), 0)
    blk = lax.broadcasted_iota(jnp.int32, (ls, 128), 1)
    cur = tokpos // SEL_LEN
    forced = (blk == 0) | (blk == cur) | (blk == cur - 1)
    sl_g = lambda g: slice(g * NSA_DH, (g + 1) * NSA_DH)
    qg = {u: (q_ref[u[0], u[1]] * NSA_SCALE).astype(BF16) for u in units}

    lg = {u: _dot_nt(qg[u], ck[u[0]][:, sl_g(u[1])].astype(BF16)) for u in units}
    pb = {u: _softmax_rows(lg[u] + bcmp_ref[u[1]]).astype(BF16) for u in units}
    o_cmp = {u: _dot(pb[u], cv[u[0]][:, sl_g(u[1])].astype(BF16)) for u in units}
    ps = {u: _dot(pb[u], ovl_ref[...]) for u in units}
    mem = {}
    for u in units:
        score = ps[u][0:ls]
        for r in range(1, NSA_HPG):
            score = score + ps[u][r * ls:(r + 1) * ls]
        sc = jnp.where(forced, jnp.inf, jnp.where(blk <= cur, score, -jnp.inf))
        member = _topk_member_lanes(sc, n_sel, n_top)
        mem[u] = jnp.concatenate([member] * NSA_HPG, axis=0).astype(BF16)
    addmask = {u: (_dot(mem[u], esel_ref[...]) - 1.0) * (-NEG) for u in units}

    lg_s = {u: _dot(qg[u], xb[u[0]][2 * gw + u[1] * NSA_DH:2 * gw + (u[1] + 1) * NSA_DH, :].astype(BF16)) for u in units}
    lg_w = {u: _dot(qg[u], wbuf[u[0], u[1] * NSA_DH:(u[1] + 1) * NSA_DH, :].astype(BF16)) for u in units}
    p_s = {u: _softmax_rows(lg_s[u] + bsel_ref[u[1]] + addmask[u]).astype(BF16) for u in units}
    p_w = {u: _softmax_rows(lg_w[u] + bwin_ref[u[1]]).astype(BF16) for u in units}
    o_sel = {u: _dot_nt(p_s[u], xb[u[0]][3 * gw + u[1] * NSA_DH:3 * gw + (u[1] + 1) * NSA_DH, :].astype(BF16))
             for u in units}
    o_win = {u: _dot_nt(p_w[u], wbuf[u[0], gw + u[1] * NSA_DH:gw + (u[1] + 1) * NSA_DH, :].astype(BF16)) for u in units}
    for u in units:
        gt = jax.nn.sigmoid(gate_ref[u[0], u[1]])
        o_ref[u[0], u[1]] = gt[:, 0:1] * o_cmp[u] + gt[:, 1:2] * o_sel[u] + gt[:, 2:3] * o_win[u]


def _decode_call(cache_t, page_table, q_s, kvn_t, winn_t, winbuf_t, gates, cw, tabs, new_win_t, *, l, past, n_sel):
    n_seq, n_pages = page_table.shape
    page = cache_t.shape[3]
    ls = kvn_t.shape[2]
    rows = NSA_HPG * ls
    w_len = winbuf_t.shape[3]
    t_cols = tabs["sel"].shape[-1]
    w_cols = tabs["win"].shape[-1]
    w1, b1, w2, b2 = cw
    gw = NSA_GROUPS * NSA_DH
    nsq = DEC_SEQ_BLOCK
    assert n_seq % nsq == 0
    n_top = min(SEL_TOP, n_sel)
    kern = functools.partial(_decode_kernel, l=l, n_pages=n_pages, page=page, ls=ls, past=past,
                             n_sel=n_sel, n_top=n_top)
    whole = lambda shape: pl.BlockSpec(shape, lambda s, pt: (0,) * len(shape))
    lay = lambda a: pl.BlockSpec((None,) + a.shape[1:], lambda s, pt: (l, 0, 0, 0))
    grid_spec = pltpu.PrefetchScalarGridSpec(
        num_scalar_prefetch=1,
        grid=(n_seq // nsq,),
        in_specs=[
            pl.BlockSpec(memory_space=pl.ANY),
            pl.BlockSpec((nsq, NSA_GROUPS, rows, NSA_DH), lambda s, pt: (s, 0, 0, 0)),
            pl.BlockSpec((nsq, 4 * gw, ls), lambda s, pt: (s, 0, 0)),
            pl.BlockSpec((nsq, 2 * gw, ls), lambda s, pt: (s, 0, 0)),
            pl.BlockSpec((None, nsq, 2 * gw, w_len), lambda s, pt: (l, s, 0, 0)),
            pl.BlockSpec((nsq, NSA_GROUPS, rows, 3), lambda s, pt: (s, 0, 0, 0)),
            lay(w1), lay(b1), lay(w2), lay(b2),
            whole(tabs["cmp"].shape), whole(tabs["sel"].shape), whole(tabs["win"].shape),
            whole(tabs["ovl"].shape), whole(tabs["esel"].shape),
            pl.BlockSpec(memory_space=pl.ANY),
        ],
        out_specs=[
            pl.BlockSpec((nsq, NSA_GROUPS, rows, NSA_DH), lambda s, pt: (s, 0, 0, 0)),
            pl.BlockSpec((None, nsq, 2 * gw, w_len), lambda s, pt: (l, s, 0, 0)),
        ],
        scratch_shapes=[pltpu.VMEM((2, nsq, 4 * gw, t_cols), F32),
                        pltpu.VMEM((nsq, 2 * gw, w_cols), F32),
                        pltpu.VMEM((nsq, 2, past, gw), F32),
                        pltpu.SemaphoreType.DMA((2,))],
    )
    return pl.pallas_call(
        kern,
        grid_spec=grid_spec,
        out_shape=[jax.ShapeDtypeStruct((n_seq, NSA_GROUPS, rows, NSA_DH), F32),
                   jax.ShapeDtypeStruct(new_win_t.shape, F32)],
        input_output_aliases={16: 1},
        compiler_params=pltpu.CompilerParams(dimension_semantics=("arbitrary",),
                                             vmem_limit_bytes=VMEM_LIMIT_BYTES),
        name="nsa_decode",
    )(page_table, cache_t, q_s, kvn_t, winn_t, winbuf_t, gates, w1, b1, w2, b2,
      tabs["cmp"], tabs["sel"], tabs["win"], tabs["ovl"], tabs["esel"], new_win_t)


def _decode_tables(rel_table, *, past, ls, n_cmp, n_sel, w_len):
    rows = NSA_HPG * ls
    t_cols = -(-(n_sel * SEL_LEN) // 128) * 128
    w_cols = -(-(w_len + ls) // 128) * 128
    n_sub = past // CMP_STRIDE
    tok = np.tile(np.arange(ls), NSA_HPG)
    hd = np.repeat(np.arange(NSA_HPG), ls)
    a = np.concatenate([past + tok, past + tok])
    head = np.concatenate([hd, NSA_HPG + hd])
    bc = np.where(np.arange(n_sub) < n_cmp, np.arange(n_sub) * CMP_STRIDE + CMP_LEN - 1, BIG)
    tabs = {
        "cmp": _bias_table(rel_table, a, bc, head, BIG).reshape(NSA_GROUPS, rows, n_sub),
        "sel": _bias_table(rel_table, a, np.arange(t_cols), head, BIG).reshape(NSA_GROUPS, rows, t_cols),
        "win": _bias_table(rel_table, a, past - w_len + np.arange(w_cols), head, WINDOW
                           ).reshape(NSA_GROUPS, rows, w_cols),
    }
    ovl = np.zeros((n_sub, 128), np.float32)
    ovl[:n_cmp, :n_sel] = _block_overlap(n_cmp, n_sel)
    esel = np.zeros((128, t_cols), np.float32)
    esel[np.arange(n_sel * SEL_LEN) // SEL_LEN, np.arange(n_sel * SEL_LEN)] = 1.0
    tabs["ovl"] = jnp.asarray(ovl, BF16)
    tabs["esel"] = jnp.asarray(esel, BF16)
    return tabs


TQ = 128
TKS = 256
SEL_OFFSETS = 3
WIN_OFFSETS = WINDOW // TQ + 2
AUG_ROWS = 32


def _prompt_tables(rel_table, *, ln):
    nq = ln // TQ
    n_sub = ln // CMP_STRIDE
    n_cmp = (ln - CMP_LEN) // CMP_STRIDE + 1
    n_sel = -(-ln // SEL_LEN)
    qq = np.tile(np.arange(TQ), NSA_HPG)
    hd = np.repeat(np.arange(NSA_HPG), TQ)
    rows = NSA_HPG * TQ
    shift = TQ // CMP_STRIDE
    width = -(-(n_sub + (nq - 1) * shift) // 256) * 256
    cmp_g = []
    for g in range(NSA_GROUPS):
        master = _bias_table_cols(rel_table, -CMP_STRIDE * (np.arange(width) - (nq - 1) * shift),
                                  (CMP_LEN - 1) - qq, NSA_HPG * g + hd, jnp.zeros((rows,), F32), BIG)
        tiles = jnp.stack([master[(nq - 1 - i) * shift:(nq - 1 - i) * shift + n_sub] for i in range(nq)])
        cmp_g.append(jnp.where((np.arange(n_sub) < n_cmp)[:, None], tiles, NEG))
    cmp_t = jnp.stack(cmp_g, axis=1)
    far = rel_table[REL_BUCKETS - 1]
    far_hi = far.astype(BF16)
    far_lo = (far - far_hi.astype(F32)).astype(BF16)
    far_eff = far_hi.astype(F32) + far_lo.astype(F32)
    sel_t, win_t, crows = [], [], []
    for g in range(NSA_GROUPS):
        head = NSA_HPG * g + hd
        a = np.concatenate([TQ * w - np.arange(TKS) for w in range(WIN_OFFSETS)])
        sub = jnp.take(far_eff, jnp.asarray(head, jnp.int32))
        near = _bias_table_cols(rel_table, a[:SEL_OFFSETS * TKS], -qq, head, sub, BIG).reshape(SEL_OFFSETS, TKS, rows)
        sel_t.append(jnp.concatenate([near, jnp.zeros((1, TKS, rows), F32)], axis=0))
        win_t.append(_bias_table_cols(rel_table, a, -qq, head, jnp.zeros((rows,), F32), WINDOW
                                      ).reshape(WIN_OFFSETS, TKS, rows))
        cr = jnp.zeros((AUG_ROWS, rows), F32)
        cr = cr.at[0].set(jnp.take(far_hi.astype(F32), jnp.asarray(head, jnp.int32)))
        cr = cr.at[1].set(jnp.take(far_lo.astype(F32), jnp.asarray(head, jnp.int32)))
        crows.append(cr)
    nb = -(-n_sel // 8) * 8
    assert nb <= AUG_ROWS
    ovl_t = np.zeros((nb, n_sub), np.float32)
    ovl_t[:n_sel, :n_cmp] = _block_overlap(n_cmp, n_sel).T
    return {"cmp": cmp_t, "sel": jnp.stack(sel_t), "win": jnp.stack(win_t), "crows": jnp.stack(crows),
            "ovl_t": jnp.asarray(ovl_t, BF16), "n_sel": n_sel}


def _pcompress_kernel(k_ref, v_ref, w1_ref, b1_ref, w2_ref, b2_ref, ck_ref, cvt_ref, *, n_sub):
    ck, cv = [_compress_rows(lambda j, src=src: src[pl.ds(j, n_sub, stride=CMP_STRIDE), :],
                             n_sub, w1_ref[k], b1_ref[k], w2_ref[k], b2_ref[k])
              for k, src in ((0, k_ref), (1, v_ref))]
    ck_ref[...] = ck
    cvt_ref[...] = cv.T


def _pcompress_call(p, cw, *, l, nb, ln):
    w1, b1, w2, b2 = cw
    n_sub = ln // CMP_STRIDE
    gw = NSA_GROUPS * NSA_DH
    lay = lambda a: pl.BlockSpec((None,) + a.shape[1:], lambda b: (l, 0, 0, 0))
    out = pl.BlockSpec((None, n_sub, gw), lambda b: (b, 0, 0))
    return pl.pallas_call(
        functools.partial(_pcompress_kernel, n_sub=n_sub),
        grid=(nb,),
        in_specs=[pl.BlockSpec((ln, gw), lambda b: (b, P_KV4 // gw)),
                  pl.BlockSpec((ln, gw), lambda b: (b, P_KV4 // gw + 1)),
                  lay(w1), lay(b1), lay(w2), lay(b2)],
        out_specs=[out, pl.BlockSpec((None, gw, n_sub), lambda b: (b, 0, 0))],
        out_shape=[jax.ShapeDtypeStruct((nb, n_sub, gw), F32), jax.ShapeDtypeStruct((nb, gw, n_sub), F32)],
        compiler_params=pltpu.CompilerParams(dimension_semantics=("parallel",),
                                             vmem_limit_bytes=VMEM_LIMIT_BYTES),
        name="nsa_compress",
    )(p, p, w1, b1, w2, b2)


def _group_queries_t(q_t, g):
    hs = [q_t[(NSA_HPG * g + r) * NSA_DH:(NSA_HPG * g + r + 1) * NSA_DH, :] for r in range(NSA_HPG)]
    return jnp.concatenate(hs, axis=1).astype(F32) * NSA_SCALE


def _pcmp_kernel(qt_ref, ck_ref, cvt_ref, bias_ref, wing_ref, ovl_ref, o_ref, mem_ref, *, n_sel, n_top):
    i = pl.program_id(1)
    gw = NSA_GROUPS * NSA_DH
    gates_t = jax.nn.sigmoid(wing_ref[:, 2 * gw:3 * gw]).T
    nbk = ovl_ref.shape[0]
    blk = lax.broadcasted_iota(jnp.int32, (nbk, TQ), 0)
    cur = (i * TQ + lax.broadcasted_iota(jnp.int32, (nbk, TQ), 1)) // SEL_LEN
    forced = (blk == 0) | (blk == cur) | (blk == cur - 1)
    q_t = qt_ref[...]
    groups = range(NSA_GROUPS)
    gsl = [slice(g * NSA_DH, (g + 1) * NSA_DH) for g in groups]
    lg = [_dot(ck_ref[:, gsl[g]].astype(BF16), _group_queries_t(q_t, g).astype(BF16)) + bias_ref[g]
          for g in groups]
    pbs = []
    for g in groups:
        m = jnp.max(lg[g], axis=0, keepdims=True)
        e = jnp.exp(lg[g] - m)
        s = jnp.sum(e, axis=0, keepdims=True)
        pbs.append(jnp.where(m > 0.5 * NEG, e / s, 0.0).astype(BF16))
    o_t = [_dot(cvt_ref[gsl[g], :].astype(BF16), pbs[g]) for g in groups]
    scores = [[_dot(ovl_ref[...], pbs[g][:, r * TQ:(r + 1) * TQ]) for r in range(NSA_HPG)] for g in groups]
    pieces = []
    for g in groups:
        score_t = functools.reduce(lambda a, b: a + b, scores[g])
        sc = jnp.where(forced, jnp.inf, jnp.where(blk <= cur, score_t, -jnp.inf))
        member_t = _topk_member_rows(sc, n_sel, n_top)
        if nbk < AUG_ROWS:
            member_t = jnp.concatenate([member_t, jnp.zeros((AUG_ROWS - nbk, TQ), F32)], axis=0)
        mem_ref[g] = member_t
        for r in range(NSA_HPG):
            h = NSA_HPG * g + r
            pieces.append(gates_t[h:h + 1] * o_t[g][:, r * TQ:(r + 1) * TQ])
    o_ref[...] = jnp.concatenate(pieces, axis=0)


def _pcmp_call(p, q_t, ck, cvt, tabs, *, nb, ln):
    nq = ln // TQ
    n_sub = ln // CMP_STRIDE
    gw = NSA_GROUPS * NSA_DH
    cols = NSA_HPG * TQ
    n_sel = tabs["n_sel"]
    return pl.pallas_call(
        functools.partial(_pcmp_kernel, n_sel=n_sel, n_top=min(SEL_TOP, n_sel)),
        grid=(nb, nq),
        in_specs=[pl.BlockSpec((None, NSA_WIDTH, TQ), lambda b, i: (b, 0, i)),
                  pl.BlockSpec((None, n_sub, gw), lambda b, i: (b, 0, 0)),
                  pl.BlockSpec((None, gw, n_sub), lambda b, i: (b, 0, 0)),
                  pl.BlockSpec((None, NSA_GROUPS, n_sub, cols), lambda b, i: (i, 0, 0, 0)),
                  pl.BlockSpec((TQ, 4 * gw), lambda b, i: (b * nq + i, P_WING // (4 * gw))),
                  pl.BlockSpec(tabs["ovl_t"].shape, lambda b, i: (0, 0))],
        out_specs=[pl.BlockSpec((None, NSA_WIDTH, TQ), lambda b, i: (b, 0, i)),
                   pl.BlockSpec((None, NSA_GROUPS, AUG_ROWS, TQ), lambda b, i: (b, 0, 0, i))],
        out_shape=[jax.ShapeDtypeStruct((nb, NSA_WIDTH, ln), F32),
                   jax.ShapeDtypeStruct((nb, NSA_GROUPS, AUG_ROWS, ln), F32)],
        compiler_params=pltpu.CompilerParams(dimension_semantics=("parallel", "arbitrary"),
                                             vmem_limit_bytes=VMEM_LIMIT_BYTES),
        name="nsa_cmp_select",
    )(q_t, ck, cvt, tabs["cmp"], p, tabs["ovl_t"])


def _tile_state(lg_t, v_aug_t, valid):
    m = jnp.max(lg_t, axis=0, keepdims=True)
    acc = _dot(v_aug_t, jnp.exp(lg_t - m).astype(BF16))
    if valid is not None:
        m = jnp.where(valid, m, NEG)
    return m, acc


def _merge_states(states):
    m = functools.reduce(jnp.maximum, [s[0] for s in states])
    acc = states[0][1] * jnp.exp(states[0][0] - m)
    for s in states[1:]:
        acc = acc + s[1] * jnp.exp(s[0] - m)
    return m, acc


def _pattn_kernel(q_ref, kv_ref, wkv_ref, wing_ref, mem_ref, ocmp_ref, bsel_ref, bwin_ref, crow_ref, o_ref,
                  kk_scr, vs_scr, vw_scr):
    i = pl.program_id(1)
    gw = NSA_GROUPS * NSA_DH
    cols = NSA_HPG * TQ
    ln = kv_ref.shape[0]

    @pl.when(i == 0)
    def _():
        pos = lax.broadcasted_iota(jnp.int32, (ln, NSA_DH), 0)
        lane = lax.broadcasted_iota(jnp.int32, (ln, NSA_DH), 1)
        extra = ((lane == pos // SEL_LEN) | (lane == AUG_ROWS) | (lane == AUG_ROWS + 1)).astype(F32)
        ones_row = (lax.broadcasted_iota(jnp.int32, (NSA_DH, ln), 0) == 0).astype(F32)
        zeros = jnp.zeros((ln, NSA_DH), F32)
        vs_t = kv_ref[:, 3 * gw:4 * gw].T
        vw_t = wkv_ref[:, gw:2 * gw].T
        for g in range(NSA_GROUPS):
            lo, hi = g * NSA_DH, (g + 1) * NSA_DH
            kk_scr[g] = jnp.concatenate([kv_ref[:, 2 * gw + lo:2 * gw + hi], extra], axis=1).astype(BF16)
            kk_scr[NSA_GROUPS + g] = jnp.concatenate([wkv_ref[:, lo:hi], zeros], axis=1).astype(BF16)
            vs_scr[g] = jnp.concatenate([vs_t[lo:hi], ones_row], axis=0).astype(BF16)
            vw_scr[g] = jnp.concatenate([vw_t[lo:hi], ones_row], axis=0).astype(BF16)

    gates_t = jax.nn.sigmoid(wing_ref[:, 2 * gw:3 * gw]).T
    q_t = q_ref[...].astype(F32) * NSA_SCALE
    m_tile = i // (TKS // TQ)
    q_augs = []
    for g in range(NSA_GROUPS):
        mask_t = (mem_ref[g] - 1.0) * (-NEG)
        q_aug = jnp.concatenate(
            [jnp.concatenate([q_t[(NSA_HPG * g + r) * NSA_DH:(NSA_HPG * g + r + 1) * NSA_DH], mask_t], axis=0)
             for r in range(NSA_HPG)], axis=1)
        q_augs.append(jnp.concatenate([q_aug, crow_ref[g]], axis=0).astype(BF16))

    def tile_states(tiles):
        lgs = []
        for branch, kt, bias_ref, _ in tiles:
            k0 = pl.multiple_of(kt * TKS, TKS)
            lg = [_dot(kk_scr[branch * NSA_GROUPS + g, pl.ds(k0, TKS), :], q_augs[g]) for g in range(NSA_GROUPS)]
            if bias_ref is not None:
                w = i - kt * (TKS // TQ)
                w = jnp.minimum(w, SEL_OFFSETS) if branch == 0 else w
                lg = [lg[g] + bias_ref[g, w] for g in range(NSA_GROUPS)]
            lgs.append(lg)
        out = []
        for (branch, kt, _, valid), lg in zip(tiles, lgs):
            k0 = pl.multiple_of(kt * TKS, TKS)
            v_scr = vs_scr if branch == 0 else vw_scr
            out.append([_tile_state(lg[g], v_scr[g, :, pl.ds(k0, TKS)], valid) for g in range(NSA_GROUPS)])
        return out

    n_far = jnp.maximum(m_tile - 1, 0)

    def far_body(n, carry):
        t1 = 2 * n + 1
        sts = tile_states([(0, 2 * n, None, None), (0, jnp.minimum(t1, n_far - 1), None, t1 < n_far)])
        return tuple(_merge_states([carry[g], sts[0][g], sts[1][g]]) for g in range(NSA_GROUPS))

    init1 = (jnp.full((1, cols), NEG, F32), jnp.zeros((2 * NSA_DH, cols), F32))
    far = lax.fori_loop(0, (n_far + 1) // 2, far_body, (init1,) * NSA_GROUPS)
    t_m1, t_m2 = jnp.maximum(m_tile - 1, 0), jnp.maximum(m_tile - 2, 0)
    sts = tile_states([(0, t_m1, bsel_ref, m_tile >= 1), (0, m_tile, bsel_ref, None),
                       (1, t_m2, bwin_ref, m_tile >= 2), (1, t_m1, bwin_ref, m_tile >= 1),
                       (1, m_tile, bwin_ref, None)])
    sel = [_merge_states([far[g], sts[0][g], sts[1][g]]) for g in range(NSA_GROUPS)]
    win = [_merge_states([sts[2][g], sts[3][g], sts[4][g]]) for g in range(NSA_GROUPS)]
    pieces = []
    for g in range(NSA_GROUPS):
        acc_s, acc_w = sel[g][1], win[g][1]
        o_s = acc_s[0:NSA_DH] / acc_s[NSA_DH:NSA_DH + 1]
        o_w = acc_w[0:NSA_DH] / acc_w[NSA_DH:NSA_DH + 1]
        for r in range(NSA_HPG):
            h = NSA_HPG * g + r
            cs = slice(r * TQ, (r + 1) * TQ)
            pieces.append(gates_t[NSA_HEADS + h:NSA_HEADS + h + 1] * o_s[:, cs]
                          + gates_t[2 * NSA_HEADS + h:2 * NSA_HEADS + h + 1] * o_w[:, cs])
    o_ref[...] = (ocmp_ref[...] + jnp.concatenate(pieces, axis=0)).T


def _pattn_call(p, q_t, mem, ocmp_t, tabs, *, nb, ln):
    nq = ln // TQ
    gw = NSA_GROUPS * NSA_DH
    tile = lambda w, cb: pl.BlockSpec((TQ, w), lambda b, i, cb=cb: (b * nq + i, cb))
    tile_t = pl.BlockSpec((None, NSA_WIDTH, TQ), lambda b, i: (b, 0, i))
    seq = lambda cb: pl.BlockSpec((ln, 4 * gw), lambda b, i, cb=cb: (b, cb))
    whole = lambda a: pl.BlockSpec(a.shape, lambda b, i: (0,) * a.ndim)
    k_scr = pltpu.VMEM((2 * NSA_GROUPS, ln, 2 * NSA_DH), BF16)
    v_scr = pltpu.VMEM((NSA_GROUPS, 2 * NSA_DH, ln), BF16)
    return pl.pallas_call(
        _pattn_kernel,
        grid=(nb, nq),
        in_specs=[tile_t, seq(P_KV4 // (4 * gw)), seq(P_WING // (4 * gw)),
                  tile(4 * gw, P_WING // (4 * gw)),
                  pl.BlockSpec((None, NSA_GROUPS, AUG_ROWS, TQ), lambda b, i: (b, 0, 0, i)),
                  tile_t,
                  whole(tabs["sel"]), whole(tabs["win"]), whole(tabs["crows"])],
        out_specs=tile(NSA_WIDTH, 0),
        out_shape=jax.ShapeDtypeStruct((nb * ln, NSA_WIDTH), F32),
        scratch_shapes=[k_scr, v_scr, v_scr],
        compiler_params=pltpu.CompilerParams(dimension_semantics=("parallel", "arbitrary"),
                                             vmem_limit_bytes=VMEM_LIMIT_BYTES),
        name="nsa_sel_win",
    )(q_t, p, p, p, mem, ocmp_t, tabs["sel"], tabs["win"], tabs["crows"])


HG_SUB = 8


def _split3(x):
    h = x.astype(BF16)
    r = x - h.astype(F32)
    m = r.astype(BF16)
    return h, m, (r - m.astype(F32)).astype(BF16)


def _row_bcast(x, row, n):
    return jnp.broadcast_to(x[row:row + 1, :], (n, x.shape[1]))


def _hgrn_prompt_kernel(q_ref, f_ref, i_ref, g_ref, lb_ref, gain_ref, tri_ref, ecat_ref, o_ref, s_ref,
                        st_scr, *, n_chunks):
    step = pl.program_id(1)
    c = HG_CHUNK
    n_sub = c // HG_SUB

    @pl.when(step == 0)
    def _():
        st_scr[...] = jnp.zeros_like(st_scr)

    lb = lb_ref[...]
    rowi = lax.broadcasted_iota(jnp.int32, (c, HG_DK), 0)
    rmod = rowi % HG_SUB
    rsub = rowi // HG_SUB
    ti = lax.broadcasted_iota(jnp.int32, (c, c), 0)
    si = lax.broadcasted_iota(jnp.int32, (c, c), 1)
    diag_mask = (ti // HG_SUB == si // HG_SUB)

    for ch in range(n_chunks):
        r0 = ch * c
        fz = f_ref[pl.ds(r0, c), :]
        sg = jax.nn.sigmoid(fz)
        logf = jnp.log(lb + (1.0 - lb) * sg)
        kk = (1.0 - lb) * jax.nn.sigmoid(-fz)
        qq = _silu(q_ref[pl.ds(r0, c), :])
        vv = i_ref[pl.ds(r0, c), :]
        gg = g_ref[pl.ds(r0, c), :]
        h0, h1, h2 = _split3(logf)
        tri = tri_ref[...]
        b_all = _dot(tri, h0) + _dot(tri, h1) + _dot(tri, h2)
        heads = range(HG_HEADS)
        hsl = [slice(h * HG_DK, (h + 1) * HG_DK) for h in heads]
        bs_ = [b_all[:, s] for s in hsl]
        qs = [qq[:, s] for s in hsl]
        ks_ = [kk[:, s] for s in hsl]
        vbs = [vv[:, s].astype(BF16) for s in hsl]
        sts = [st_scr[h] for h in heads]
        qes, lhss, rhss, zcats, kdecs, elast = [], [], [], [], [], []
        for h in heads:
            b, q, k = bs_[h], qs[h], ks_[h]
            qes.append((q * jnp.exp(b)).astype(BF16))
            bstart = jnp.concatenate(
                [jnp.zeros((HG_SUB, HG_DK), F32)]
                + [_row_bcast(b, HG_SUB * i - 1, HG_SUB) for i in range(1, n_sub)], axis=0)
            bend = jnp.concatenate([_row_bcast(b, HG_SUB * (j + 1) - 1, HG_SUB) for j in range(n_sub)], axis=0)
            qh = q * jnp.exp(b - bstart)
            kh = k * jnp.exp(bend - b)
            lhs, rhs = [], []
            for j in range(n_sub - 1):
                bj = b[HG_SUB * (j + 1) - 1:HG_SUB * (j + 1), :]
                dj = jnp.exp(jnp.minimum(bstart - bj, 0.0))
                lhs.append(jnp.where(rsub > j, qh * dj, 0.0).astype(BF16))
                rhs.append(jnp.where(rsub == j, kh, 0.0).astype(BF16))
            lhss.append(jnp.concatenate(lhs, axis=1))
            rhss.append(jnp.concatenate(rhs, axis=1))
            zs = []
            for s in range(HG_SUB):
                ksb = jnp.concatenate([_row_bcast(k, HG_SUB * i + s, HG_SUB) for i in range(n_sub)], axis=0)
                bsb = jnp.concatenate([_row_bcast(b, HG_SUB * i + s, HG_SUB) for i in range(n_sub)], axis=0)
                dec = jnp.where(rmod >= s, jnp.exp(b - bsb), 0.0)
                zs.append((q * ksb * dec).astype(BF16))
            zcats.append(jnp.concatenate(zs, axis=1))
            blast = b[c - 1:c, :]
            kdecs.append((k * jnp.exp(blast - b)).astype(BF16))
            elast.append(jnp.exp(blast))
        o_in = [_dot_nt(qes[h], sts[h].astype(BF16)) for h in heads]
        a_off = [_dot_nt(lhss[h], rhss[h]) for h in heads]
        a_diag = [_dot(zcats[h], ecat_ref[...]) for h in heads]
        upd = [lax.dot_general(vbs[h], kdecs[h], (((0,), (0,)), ((), ())), preferred_element_type=F32)
               for h in heads]
        for h in heads:
            st_scr[h] = sts[h] * elast[h] + upd[h]
        a = [(a_off[h] + jnp.where(diag_mask, a_diag[h], 0.0)).astype(BF16) for h in heads]
        o_intra = [_dot(a[h], vbs[h]) for h in heads]
        outs = []
        for h in heads:
            o = o_in[h] + o_intra[h]
            y = o * lax.rsqrt(jnp.mean(o * o, axis=-1, keepdims=True) + EPS) * gain_ref[...]
            outs.append(y * _silu(gg[:, hsl[h]]))
        o_ref[pl.ds(r0, c), :] = jnp.concatenate(outs, axis=1)

    @pl.when(step == pl.num_programs(1) - 1)
    def _():
        for h in range(HG_HEADS):
            s_ref[h] = st_scr[h].T


def _hgrn_prompt_call(p, lb_l, gain_l, *, nb, ln, tr):
    c = HG_CHUNK
    w = HG_HEADS * HG_DK
    steps = ln // tr
    tri = jnp.asarray(np.tril(np.ones((c, c), np.float32)), BF16)
    ecat = np.zeros((HG_SUB * HG_DK, c), np.float32)
    for s in range(HG_SUB):
        ecat[s * HG_DK:(s + 1) * HG_DK, s::HG_SUB] = 1.0
    col = lambda off: pl.BlockSpec((tr, w), lambda b, i, off=off: (b * steps + i, off // w))
    return pl.pallas_call(
        functools.partial(_hgrn_prompt_kernel, n_chunks=tr // c),
        grid=(nb, steps),
        in_specs=[col(P_QB), col(P_FB), col(P_IB), col(P_GB),
                  pl.BlockSpec((1, w), lambda b, i: (0, 0)),
                  pl.BlockSpec((1, HG_DV), lambda b, i: (0, 0)),
                  pl.BlockSpec((c, c), lambda b, i: (0, 0)),
                  pl.BlockSpec((HG_SUB * HG_DK, c), lambda b, i: (0, 0))],
        out_specs=[pl.BlockSpec((tr, w), lambda b, i: (b * steps + i, 0)),
                   pl.BlockSpec((None, HG_HEADS, HG_DK, HG_DV), lambda b, i: (b, 0, 0, 0))],
        out_shape=[jax.ShapeDtypeStruct((nb * ln, w), F32),
                   jax.ShapeDtypeStruct((nb, HG_HEADS, HG_DK, HG_DV), F32)],
        scratch_shapes=[pltpu.VMEM((HG_HEADS, HG_DV, HG_DK), F32)],
        compiler_params=pltpu.CompilerParams(dimension_semantics=("parallel", "arbitrary"),
                                             vmem_limit_bytes=VMEM_LIMIT_BYTES),
        name="hgrn_prompt",
    )(p, p, p, p, lb_l.reshape(1, w), gain_l.reshape(1, HG_DV), tri, jnp.asarray(ecat, BF16))


HG_SEQ_BLOCK = 4


def _hgrn_decode_kernel(q_ref, f_ref, i_ref, g_ref, lb_ref, gain_ref, s0_ref, _, o_ref, s_ref, *, ls):
    lb = lb_ref[...]
    rowi = lax.broadcasted_iota(jnp.int32, (8, HG_DK), 0)
    zpad = jnp.zeros((8 - ls, HG_DK), F32)
    for n in range(q_ref.shape[0]):
        fz = f_ref[n]
        logf = jnp.log(lb + (1.0 - lb) * jax.nn.sigmoid(fz))
        kk = (1.0 - lb) * jax.nn.sigmoid(-fz)
        qq = _silu(q_ref[n])
        vv = i_ref[n]
        gg = g_ref[n]
        brows = [logf[0:1]]
        for t in range(1, ls):
            brows.append(brows[-1] + logf[t:t + 1])
        b_all = jnp.concatenate(brows, axis=0)
        outs = []
        for h in range(HG_HEADS):
            sl = slice(h * HG_DK, (h + 1) * HG_DK)
            b, q, k, v = b_all[:, sl], qq[:, sl], kk[:, sl], vv[:, sl]
            st = s0_ref[n, h]
            blast = b[ls - 1:ls]
            e_last = jnp.exp(blast)
            e_hi = e_last.astype(BF16)
            e_lo = (e_last - e_hi.astype(F32)).astype(BF16)
            lhs = jnp.concatenate([(k * jnp.exp(blast - b)).astype(BF16), e_hi, e_lo,
                                   jnp.zeros((8 - ls - 2, HG_DK), BF16)], axis=0)
            ones2 = ((rowi >= ls) & (rowi < ls + 2)).astype(F32)
            rhs = jnp.concatenate([jnp.concatenate([v, zpad], axis=0), ones2], axis=1).astype(BF16)
            upd = lax.dot_general(lhs, rhs, (((0,), (0,)), ((), ())), preferred_element_type=F32)
            s_ref[n, h] = upd[:, HG_DV:] * st + upd[:, :HG_DV]
            o = _dot(jnp.concatenate([q * jnp.exp(b), zpad], axis=0).astype(BF16), st.astype(BF16))[0:ls]
            for s in range(ls):
                keep = rowi[0:ls] >= s
                dec = jnp.where(keep, jnp.exp(jnp.minimum(b - b[s:s + 1], 0.0)), 0.0)
                a_s = jnp.sum(q * k[s:s + 1] * dec, axis=-1, keepdims=True)
                o = o + a_s * v[s:s + 1]
            y = o * lax.rsqrt(jnp.mean(o * o, axis=-1, keepdims=True) + EPS) * gain_ref[...]
            outs.append(y * _silu(gg[:, sl]))
        o_ref[n] = jnp.concatenate(outs, axis=1)


def _hgrn_decode_call(qb, fb, ib, gb, lb_l, gain_l, state, new_state, *, l):
    n_seq, ls, w = qb.shape
    sb = HG_SEQ_BLOCK
    assert ls + 2 <= 8 and n_seq % sb == 0
    tok = pl.BlockSpec((sb, ls, w), lambda s: (s, 0, 0))
    layer = pl.BlockSpec((None, sb, HG_HEADS, HG_DK, HG_DV), lambda s: (l, s, 0, 0, 0))
    return pl.pallas_call(
        functools.partial(_hgrn_decode_kernel, ls=ls),
        grid=(n_seq // sb,),
        in_specs=[tok, tok, tok, tok,
                  pl.BlockSpec((1, w), lambda s: (0, 0)),
                  pl.BlockSpec((1, HG_DV), lambda s: (0, 0)),
                  layer, pl.BlockSpec(memory_space=pl.ANY)],
        out_specs=[tok, layer],
        out_shape=[jax.ShapeDtypeStruct((n_seq, ls, w), F32),
                   jax.ShapeDtypeStruct(new_state.shape, F32)],
        input_output_aliases={7: 1},
        compiler_params=pltpu.CompilerParams(dimension_semantics=("parallel",)),
        name="hgrn_decode",
    )(qb, fb, ib, gb, lb_l.reshape(1, w), gain_l.reshape(1, HG_DV), state, new_state)


def kernel(x_prompt, x_sample, cache_nsa_kv, state_win_kv, state_hgrn, page_table, c_prompt, c_sample,
           norm_g, ada_w, ada_b, ffn_w_in, ffn_w_out, w_in, cmp_w1, cmp_b1, cmp_w2, cmp_b2, rel_table,
           hgrn_lb_logits, hgrn_norm_g, w_branch_nsa, w_branch_hgrn, w_out, final_g):
    depth = norm_g.shape[0]
    bp, lp, d = x_prompt.shape
    bs, ls, _ = x_sample.shape
    n_seq, n_pages = page_table.shape
    n_pool, page = cache_nsa_kv.shape[1:3]
    past = n_pages * page
    w_len = state_win_kv.shape[2]
    gw = NSA_GROUPS * NSA_DH
    assert bs == 128 and n_seq == bs, "sample rows are tiled per 128 sequences"
    assert lp % TKS == 0 and lp >= WINDOW and (lp // CMP_STRIDE) % 128 == 0
    n_cmp_s = (past + ls - CMP_LEN) // CMP_STRIDE + 1
    n_sel_s = -(-(past + ls) // SEL_LEN)
    assert (n_cmp_s - 1) * CMP_STRIDE + CMP_LEN <= past and (past // CMP_STRIDE) % 128 == 0
    assert past % SEL_LEN == 0 and ls <= 8 and past // SEL_LEN == (past + ls - 1) // SEL_LEN

    lb = jnp.cumsum(jax.nn.softmax(hgrn_lb_logits.astype(F32), axis=0), axis=0)
    lb = lb - lb[:1]

    w_in_p, wq_t, w_gate = _split_w_in(w_in)
    wpa_b = w_branch_nsa.astype(BF16)
    wpb_b = w_branch_hgrn.astype(BF16)
    wo_b = w_out.astype(BF16)
    cw = _compress_weights(cmp_w1, cmp_b1, cmp_w2, cmp_b2)
    tabs_p = _prompt_tables(rel_table, ln=lp)
    tabs_s = _decode_tables(rel_table, past=past, ls=ls, n_cmp=n_cmp_s, n_sel=n_sel_s, w_len=w_len)
    cache_t = cache_nsa_kv.reshape(depth, n_pool, page, 4 * gw).transpose(0, 1, 3, 2)
    winbuf_t = state_win_kv.reshape(depth, bs, w_len, 2 * gw).transpose(0, 1, 3, 2)

    mods = _mods_call(jnp.concatenate([c_sample, c_prompt], axis=0), ada_w, ada_b)

    def sublayers(x2, l, tm, seq_tiles, mixers):
        x2 = _ffn_call(x2, norm_g, mods, ffn_w_in, ffn_w_out, final_g, l=l, s=0, sub=0, tm=tm, tf=256,
                       seq_tiles=seq_tiles, final=False)
        tm_proj = tm if seq_tiles is None else tm * seq_tiles[1]
        p, q_t = _proj_call(x2, norm_g, mods, w_in_p, wq_t, l=l, tm=tm_proj, tn=512,
                            seq_tiles=None if seq_tiles is None else (seq_tiles[0], 1))
        o_a, o_b, extras = mixers(p, q_t)
        x2 = _merge_call(x2, o_a, o_b, norm_g, mods, w_gate, wpa_b, wpb_b, wo_b, l=l, tm=tm, seq_tiles=seq_tiles)
        x2 = _ffn_call(x2, norm_g, mods, ffn_w_in, ffn_w_out, final_g, l=l, s=1, sub=2, tm=tm, tf=256,
                       seq_tiles=seq_tiles, final=(l == depth - 1))
        return x2, extras

    tm_p = 1024 if lp % 1024 == 0 else lp
    x2 = x_prompt.reshape(bp * lp, d)
    kv_p, win_p, hg_p = [], [], []
    for l in range(depth):
        def prompt_mixers(p, q_t, l=l):
            ck, cvt = _pcompress_call(p, cw, l=l, nb=bp, ln=lp)
            ocmp_t, mem = _pcmp_call(p, q_t, ck, cvt, tabs_p, nb=bp, ln=lp)
            o_a = _pattn_call(p, q_t, mem, ocmp_t, tabs_p, nb=bp, ln=lp)
            o_b, s_new = _hgrn_prompt_call(p, lb[l], hgrn_norm_g[l], nb=bp, ln=lp, tr=256)
            p3 = p.reshape(bp, lp, P_TOTAL)
            kv4 = p3[:, :, P_KV4:P_KV4 + 4 * gw].reshape(bp, lp, 4, NSA_GROUPS, NSA_DH)
            nwin = p3[:, lp - min(WINDOW, lp):, P_WING:P_WING + 2 * gw].reshape(bp, -1, 2, NSA_GROUPS, NSA_DH)
            return o_a, o_b, (kv4, nwin, s_new)

        x2, (kv4, nwin, s_new) = sublayers(x2, l, tm_p, (bs, lp // tm_p), prompt_mixers)
        kv_p.append(kv4)
        win_p.append(nwin)
        hg_p.append(s_new)
    y_p = x2.reshape(bp, lp, d)

    x2 = x_sample.transpose(1, 0, 2).reshape(ls * bs, d)
    kv_s = []
    win_acc = jnp.zeros(winbuf_t.shape, F32)
    hg_acc = jnp.zeros(state_hgrn.shape, F32)
    for l in range(depth):
        def sample_mixers(p, q_t, win_acc, hg_acc, l=l):
            ps = p.reshape(ls, bs, P_TOTAL).transpose(1, 0, 2)
            q_s = q_t.astype(F32).reshape(NSA_GROUPS, NSA_HPG, NSA_DH, ls, bs)
            q_s = q_s.transpose(4, 0, 1, 3, 2).reshape(bs, NSA_GROUPS, NSA_HPG * ls, NSA_DH)
            ga = ps[:, :, P_WING + 2 * gw:P_WING + 2 * gw + 3 * NSA_HEADS].reshape(bs, ls, 3, NSA_GROUPS, NSA_HPG)
            ga = ga.transpose(0, 3, 4, 1, 2).reshape(bs, NSA_GROUPS, NSA_HPG * ls, 3)
            kvn = ps[:, :, P_KV4:P_KV4 + 4 * gw]
            winn = ps[:, :, P_WING:P_WING + 2 * gw]
            o, win_acc = _decode_call(cache_t, page_table, q_s, kvn.transpose(0, 2, 1), winn.transpose(0, 2, 1),
                                      winbuf_t, ga, cw, tabs_s, win_acc, l=l, past=past, n_sel=n_sel_s)
            o_a = o.reshape(bs, NSA_GROUPS, NSA_HPG, ls, NSA_DH).transpose(3, 0, 1, 2, 4).reshape(ls * bs, NSA_WIDTH)
            o_b, hg_acc = _hgrn_decode_call(ps[:, :, P_QB:P_QB + 512], ps[:, :, P_FB:P_FB + 512],
                                            ps[:, :, P_IB:P_IB + 512], ps[:, :, P_GB:P_GB + 512],
                                            lb[l], hgrn_norm_g[l], state_hgrn, hg_acc, l=l)
            o_b = o_b.transpose(1, 0, 2).reshape(ls * bs, HG_WIDTH)
            return o_a, o_b, (kvn.reshape(bs, ls, 4, NSA_GROUPS, NSA_DH), win_acc, hg_acc)

        x2, (kv4, win_acc, hg_acc) = sublayers(
            x2, l, ls * bs, None, functools.partial(sample_mixers, win_acc=win_acc, hg_acc=hg_acc))
        kv_s.append(kv4)
    y_s = x2.reshape(ls, bs, d).transpose(1, 0, 2)
    win_s = win_acc.transpose(0, 1, 3, 2).reshape(depth, bs, w_len, 2, NSA_GROUPS, NSA_DH)

    return (y_p, y_s, jnp.stack(kv_p), jnp.stack(kv_s), jnp.stack(win_p), win_s, jnp.stack(hg_p), hg_acc)
```

```python
import functools
import math

import jax
import jax.numpy as jnp
import numpy as np
from jax import lax
from jax.experimental import pallas as pl
from jax.experimental.pallas import tpu as pltpu

NSA_HEADS = 8
NSA_GROUPS = 2
NSA_HPG = NSA_HEADS // NSA_GROUPS
NSA_DH = 64
NSA_WIDTH = NSA_HEADS * NSA_DH
NSA_SCALE = NSA_DH ** -0.5
CMP_LEN = 32
CMP_STRIDE = 16
CMP_RATIO = CMP_LEN // CMP_STRIDE
CMP_HID = 128
SEL_LEN = 64
SEL_TOP = 16
WINDOW = 512
HG_HEADS = 4
HG_DK = 128
HG_DV = 128
HG_WIDTH = HG_HEADS * HG_DV
HG_CHUNK = 64
REL_BUCKETS = 32
REL_MAX_DIST = 128
EPS = 1e-6

BF16 = jnp.bfloat16
F32 = jnp.float32

VMEM_LIMIT_BYTES = 56 * 1024 * 1024

P_KV4, P_WING, P_QB, P_FB, P_IB, P_GB, P_TOTAL = (0, 512, 1024, 1536, 2048, 2560, 3072)


def _split_w_in(w_in):
    d_model = w_in.shape[1]
    sizes = (NSA_WIDTH, 6 * NSA_GROUPS * NSA_DH, 3 * NSA_HEADS, HG_HEADS * HG_DK, HG_HEADS * HG_DK,
             HG_WIDTH, HG_WIDTH, d_model, d_model)
    offs = np.concatenate([[0], np.cumsum(sizes)])
    qa, kva, ga, qb, fb, ib, gb, ma, mb = [w_in[:, :, int(offs[i]):int(offs[i + 1])] for i in range(9)]
    pad = jnp.zeros(w_in.shape[:2] + (P_QB - P_KV4 - kva.shape[2] - ga.shape[2],), w_in.dtype)
    out = jnp.concatenate([kva, ga, pad, qb, fb, ib, gb], axis=2)
    assert out.shape[2] == P_TOTAL
    return (out.astype(BF16), qa.transpose(0, 2, 1).astype(BF16),
            jnp.concatenate([ma, mb], axis=2).astype(BF16))


def _mods_kernel(c_ref, w_ref, b_ref, o_ref):
    c = c_ref[...]
    sc = (c * jax.nn.sigmoid(c)).astype(BF16)
    o_ref[...] = jnp.dot(sc, w_ref[...].astype(BF16), preferred_element_type=F32) + b_ref[...]


def _mods_call(c_all, ada_w, ada_b):
    depth, d, nd = ada_w.shape
    n = c_all.shape[0]
    b3 = ada_b.reshape(depth, 1, nd)
    return pl.pallas_call(
        _mods_kernel,
        grid=(depth, nd // d),
        in_specs=[
            pl.BlockSpec((n, d), lambda l, j: (0, 0)),
            pl.BlockSpec((None, d, d), lambda l, j: (l, 0, j)),
            pl.BlockSpec((None, 1, d), lambda l, j: (l, 0, j)),
        ],
        out_specs=pl.BlockSpec((None, None, n, d), lambda l, j: (l, j, 0, 0)),
        out_shape=jax.ShapeDtypeStruct((depth, nd // d, n, d), F32),
        compiler_params=pltpu.CompilerParams(dimension_semantics=("arbitrary", "arbitrary")),
        name="adaln_mods",
    )(c_all, ada_w, b3)


def _rows(m, tm):
    r = m.shape[0]
    if r == 1 or r == tm:
        return m
    return jnp.concatenate([m] * (tm // r), axis=0)


def _norm_mod(x, g, shift, scale):
    y = x * lax.rsqrt(jnp.mean(x * x, axis=-1, keepdims=True) + EPS) * g
    return y * (1.0 + scale) + shift


def _mod_spec(mods, l, k, tm, seq_tiles):
    if seq_tiles is not None:
        row0, tiles_per_seq = seq_tiles
        d = mods.shape[-1]
        m5 = mods.reshape(mods.shape[0], mods.shape[1], mods.shape[2], 1, d)
        return m5, pl.BlockSpec((None, None, None, 1, d),
                                lambda i, j, l=l, k=k: (l, k, row0 + i // tiles_per_seq, 0, 0))
    d = mods.shape[-1]
    return mods, pl.BlockSpec((None, None, 128, d), lambda i, j, l=l, k=k: (l, k, 0, 0))


def _ffn_kernel(x_ref, g_ref, sh_ref, sc_ref, gt_ref, wa_ref, wb_ref, wo_ref, fg_ref, o_ref,
                n_scr, acc_scr, *, nf, final):
    j = pl.program_id(1)
    tm = x_ref.shape[0]

    @pl.when(j == 0)
    def _():
        n = _norm_mod(x_ref[...], g_ref[...], _rows(sh_ref[...], tm), _rows(sc_ref[...], tm))
        n_scr[...] = n.astype(BF16)
        acc_scr[...] = jnp.zeros_like(acc_scr)

    n = n_scr[...]
    a = jnp.dot(n, wa_ref[...].astype(BF16), preferred_element_type=F32)
    b = jnp.dot(n, wb_ref[...].astype(BF16), preferred_element_type=F32)
    h = (a * jax.nn.sigmoid(a) * b).astype(BF16)
    acc_scr[...] += jnp.dot(h, wo_ref[...].astype(BF16), preferred_element_type=F32)

    @pl.when(j == nf - 1)
    def _():
        y = x_ref[...] + 0.5 * _rows(gt_ref[...], tm) * acc_scr[...]
        if final:
            y = y * lax.rsqrt(jnp.mean(y * y, axis=-1, keepdims=True) + EPS) * fg_ref[...]
        o_ref[...] = y


def _ffn_call(x, norm_g, mods, ffn_w_in, ffn_w_out, final_g, *, l, s, sub, tm, tf, seq_tiles, final):
    t, d = x.shape
    d_ff = ffn_w_out.shape[2]
    nf = d_ff // tf
    g2 = norm_g[l, sub].reshape(1, d)
    fg2 = final_g.reshape(1, d)
    m_sh, sp_sh = _mod_spec(mods, l, 3 * sub, tm, seq_tiles)
    m_sc, sp_sc = _mod_spec(mods, l, 3 * sub + 1, tm, seq_tiles)
    m_gt, sp_gt = _mod_spec(mods, l, 3 * sub + 2, tm, seq_tiles)
    return pl.pallas_call(
        functools.partial(_ffn_kernel, nf=nf, final=final),
        grid=(t // tm, nf),
        in_specs=[
            pl.BlockSpec((tm, d), lambda i, j: (i, 0)),
            pl.BlockSpec((1, d), lambda i, j: (0, 0)),
            sp_sh, sp_sc, sp_gt,
            pl.BlockSpec((None, None, d, tf), lambda i, j: (l, s, 0, j)),
            pl.BlockSpec((None, None, d, tf), lambda i, j: (l, s, 0, nf + j)),
            pl.BlockSpec((None, None, tf, d), lambda i, j: (l, s, j, 0)),
            pl.BlockSpec((1, d), lambda i, j: (0, 0)),
        ],
        out_specs=pl.BlockSpec((tm, d), lambda i, j: (i, 0)),
        out_shape=jax.ShapeDtypeStruct((t, d), F32),
        scratch_shapes=[pltpu.VMEM((tm, d), BF16), pltpu.VMEM((tm, d), F32)],
        compiler_params=pltpu.CompilerParams(
            dimension_semantics=("parallel", "arbitrary"), vmem_limit_bytes=VMEM_LIMIT_BYTES),
        name="ffn",
    )(x, g2, m_sh, m_sc, m_gt, ffn_w_in, ffn_w_in, ffn_w_out, fg2)


def _proj_kernel(x_ref, g_ref, sh_ref, sc_ref, w_ref, wq_ref, o_ref, qt_ref, n_scr):
    j = pl.program_id(1)
    tm = x_ref.shape[0]

    @pl.when(j == 0)
    def _():
        n = _norm_mod(x_ref[...], g_ref[...], _rows(sh_ref[...], tm), _rows(sc_ref[...], tm))
        n_scr[...] = n.astype(BF16)
        qt_ref[...] = _dot_nt(wq_ref[...], n_scr[...]).astype(BF16)

    o_ref[...] = jnp.dot(n_scr[...], w_ref[...], preferred_element_type=F32)


def _proj_call(x, norm_g, mods, w_in_p, wq_t, *, l, tm, tn, seq_tiles):
    t, d = x.shape
    n_out = w_in_p.shape[2]
    g2 = norm_g[l, 1].reshape(1, d)
    m_sh, sp_sh = _mod_spec(mods, l, 3, tm, seq_tiles)
    m_sc, sp_sc = _mod_spec(mods, l, 4, tm, seq_tiles)
    return pl.pallas_call(
        _proj_kernel,
        grid=(t // tm, n_out // tn),
        in_specs=[
            pl.BlockSpec((tm, d), lambda i, j: (i, 0)),
            pl.BlockSpec((1, d), lambda i, j: (0, 0)),
            sp_sh, sp_sc,
            pl.BlockSpec((None, d, tn), lambda i, j: (l, 0, j)),
            pl.BlockSpec((None, NSA_WIDTH, d), lambda i, j: (l, 0, 0)),
        ],
        out_specs=[pl.BlockSpec((tm, tn), lambda i, j: (i, j)),
                   pl.BlockSpec((None, NSA_WIDTH, tm), lambda i, j: (i, 0, 0))],
        out_shape=[jax.ShapeDtypeStruct((t, n_out), F32),
                   jax.ShapeDtypeStruct((t // tm, NSA_WIDTH, tm), BF16)],
        scratch_shapes=[pltpu.VMEM((tm, d), BF16)],
        compiler_params=pltpu.CompilerParams(
            dimension_semantics=("parallel", "arbitrary"), vmem_limit_bytes=VMEM_LIMIT_BYTES),
        name="in_proj",
    )(x, g2, m_sh, m_sc, w_in_p, wq_t)


def _merge_kernel(x_ref, oa_ref, ob_ref, g_ref, sh_ref, sc_ref, gt_ref, wm_ref, wpa_ref, wpb_ref, wo_ref, o_ref):
    tm, d = x_ref.shape
    x = x_ref[...]
    n = _norm_mod(x, g_ref[...], _rows(sh_ref[...], tm), _rows(sc_ref[...], tm)).astype(BF16)
    ma = jnp.dot(n, wm_ref[:, 0:d], preferred_element_type=F32)
    mb = jnp.dot(n, wm_ref[:, d:2 * d], preferred_element_type=F32)
    ya = jnp.dot(oa_ref[...].astype(BF16), wpa_ref[...], preferred_element_type=F32)
    yb = jnp.dot(ob_ref[...].astype(BF16), wpb_ref[...], preferred_element_type=F32)
    merged = jax.nn.sigmoid(ma) * ya + jax.nn.sigmoid(mb) * yb
    y = jnp.dot(merged.astype(BF16), wo_ref[...], preferred_element_type=F32)
    o_ref[...] = x + _rows(gt_ref[...], tm) * y


def _merge_call(x, o_a, o_b, norm_g, mods, w_gate, wpa, wpb, wo, *, l, tm, seq_tiles):
    t, d = x.shape
    g2 = norm_g[l, 1].reshape(1, d)
    m_sh, sp_sh = _mod_spec(mods, l, 3, tm, seq_tiles)
    m_sc, sp_sc = _mod_spec(mods, l, 4, tm, seq_tiles)
    m_gt, sp_gt = _mod_spec(mods, l, 5, tm, seq_tiles)
    one = lambda i, j: (i, 0)
    return pl.pallas_call(
        _merge_kernel,
        grid=(t // tm, 1),
        in_specs=[
            pl.BlockSpec((tm, d), one),
            pl.BlockSpec((tm, NSA_WIDTH), one),
            pl.BlockSpec((tm, HG_WIDTH), one),
            pl.BlockSpec((1, d), lambda i, j: (0, 0)),
            sp_sh, sp_sc, sp_gt,
            pl.BlockSpec((None, d, 2 * d), lambda i, j: (l, 0, 0)),
            pl.BlockSpec((None, NSA_WIDTH, d), lambda i, j: (l, 0, 0)),
            pl.BlockSpec((None, HG_WIDTH, d), lambda i, j: (l, 0, 0)),
            pl.BlockSpec((None, d, d), lambda i, j: (l, 0, 0)),
        ],
        out_specs=pl.BlockSpec((tm, d), one),
        out_shape=jax.ShapeDtypeStruct((t, d), F32),
        compiler_params=pltpu.CompilerParams(
            dimension_semantics=("parallel", "arbitrary"), vmem_limit_bytes=VMEM_LIMIT_BYTES),
        name="merge_out",
    )(x, o_a, o_b, g2, m_sh, m_sc, m_gt, w_gate, wpa, wpb, wo)


NEG = -1e30


def _bucket_thresholds():
    n = np.arange(0, 4 * REL_MAX_DIST)
    exact = REL_BUCKETS // 2
    large = exact + np.floor(np.log(np.maximum(n, 1) / exact) / math.log(REL_MAX_DIST / exact)
                             * (REL_BUCKETS - exact)).astype(np.int64)
    bucket = np.where(n < exact, n, np.minimum(large, REL_BUCKETS - 1))
    return [int(np.min(n[bucket >= k])) for k in range(REL_BUCKETS)]


_THRESHOLDS = _bucket_thresholds()


def _bias_kernel(tab_ref, a_ref, b_ref, o_ref, *, hi):
    dist = a_ref[...] - b_ref[...]
    tab = tab_ref[...]
    v = jnp.broadcast_to(tab[:, REL_BUCKETS - 1:REL_BUCKETS], dist.shape)
    for k in range(REL_BUCKETS - 2, -1, -1):
        v = jnp.where(dist < _THRESHOLDS[k + 1], tab[:, k:k + 1], v)
    o_ref[...] = jnp.where((dist >= 0) & (dist < hi), v, NEG)


def _bias_table(rel_table, a, b, head, hi):
    r, c = a.shape[0], b.shape[0]
    tr = min(r, 512)
    assert r % tr == 0
    tab = jnp.take(rel_table, jnp.asarray(head, jnp.int32), axis=1).T
    return pl.pallas_call(
        functools.partial(_bias_kernel, hi=hi),
        grid=(r // tr,),
        in_specs=[pl.BlockSpec((tr, REL_BUCKETS), lambda i: (i, 0)),
                  pl.BlockSpec((tr, 1), lambda i: (i, 0)),
                  pl.BlockSpec((1, c), lambda i: (0, 0))],
        out_specs=pl.BlockSpec((tr, c), lambda i: (i, 0)),
        out_shape=jax.ShapeDtypeStruct((r, c), F32),
        name="rel_bias_table",
    )(tab, jnp.asarray(a, jnp.int32).reshape(r, 1), jnp.asarray(b, jnp.int32).reshape(1, c))


def _bias_cols_kernel(tab_ref, a_ref, b_ref, sub_ref, o_ref, *, hi):
    dist = a_ref[...] - b_ref[...]
    tab = tab_ref[...]
    v = jnp.broadcast_to(tab[REL_BUCKETS - 1:REL_BUCKETS, :], dist.shape)
    for k in range(REL_BUCKETS - 2, -1, -1):
        v = jnp.where(dist < _THRESHOLDS[k + 1], tab[k:k + 1, :], v)
    o_ref[...] = jnp.where((dist >= 0) & (dist < hi), v - sub_ref[...], NEG)


def _bias_table_cols(rel_table, a, b, head, sub, hi):
    r, c = a.shape[0], b.shape[0]
    tr = min(r, 256)
    assert r % tr == 0
    tab = jnp.take(rel_table, jnp.asarray(head, jnp.int32), axis=1)
    return pl.pallas_call(
        functools.partial(_bias_cols_kernel, hi=hi),
        grid=(r // tr,),
        in_specs=[pl.BlockSpec((REL_BUCKETS, c), lambda i: (0, 0)),
                  pl.BlockSpec((tr, 1), lambda i: (i, 0)),
                  pl.BlockSpec((1, c), lambda i: (0, 0)),
                  pl.BlockSpec((1, c), lambda i: (0, 0))],
        out_specs=pl.BlockSpec((tr, c), lambda i: (i, 0)),
        out_shape=jax.ShapeDtypeStruct((r, c), F32),
        name="rel_bias_table_t",
    )(tab, jnp.asarray(a, jnp.int32).reshape(r, 1), jnp.asarray(b, jnp.int32).reshape(1, c), sub.reshape(1, c))


BIG = 1 << 30


def _dot_nt(a, b):
    return lax.dot_general(a, b, (((1,), (1,)), ((), ())), preferred_element_type=F32)


def _dot(a, b):
    return jnp.dot(a, b, preferred_element_type=F32)


def _silu(x):
    return x * jax.nn.sigmoid(x)


def _block_overlap(n_cmp, n_sel):
    c0 = np.arange(n_cmp) * CMP_STRIDE
    s0 = np.arange(n_sel) * SEL_LEN
    return ((c0[:, None] <= s0[None] + SEL_LEN - 1) & (c0[:, None] + CMP_LEN - 1 >= s0[None])).astype(np.float32)


def _compress_weights(cmp_w1, cmp_b1, cmp_w2, cmp_b2):
    eye = jnp.eye(NSA_GROUPS, dtype=F32)
    w1 = jnp.einsum('lkrjde,gh->lkjgdrhe', cmp_w1, eye)
    depth = cmp_w1.shape[0]
    w1 = w1.reshape(depth, 2, CMP_STRIDE * NSA_GROUPS * NSA_DH, CMP_RATIO * NSA_GROUPS * CMP_HID).astype(BF16)
    w2 = jnp.einsum('lked,gh->lkgehd', cmp_w2, eye).reshape(depth, 2, NSA_GROUPS * CMP_HID, NSA_GROUPS * NSA_DH)
    b1 = jnp.tile(cmp_b1, (1, 1, NSA_GROUPS)).reshape(depth, 2, 1, NSA_GROUPS * CMP_HID)
    b2 = jnp.tile(cmp_b2, (1, 1, NSA_GROUPS)).reshape(depth, 2, 1, NSA_GROUPS * NSA_DH)
    return w1, b1, w2.astype(BF16), b2


def _compress_rows(load_piece, n_sub, w1, b1, w2, b2):
    x = jnp.concatenate([load_piece(j).astype(BF16) for j in range(CMP_STRIDE)], axis=1)
    hr = _dot(x, w1)
    half = NSA_GROUPS * CMP_HID
    h = hr[:, :half] + pltpu.roll(hr[:, half:], n_sub - 1, 0) + b1
    return _dot(_silu(h).astype(BF16), w2) + b2


def _topk_member_rows(sc, n_sel, n_top):
    blk = lax.broadcasted_iota(jnp.int32, sc.shape, 0)
    cnt = jnp.zeros(sc.shape, jnp.int32)
    for i in range(n_sel):
        si = sc[i:i + 1, :]
        ahead = (si > sc) | ((si == sc) & (blk > i))
        cnt = cnt + ahead.astype(jnp.int32)
    return ((cnt < n_top) & (blk < n_sel)).astype(F32)


def _topk_member_lanes(sc, n_sel, n_top):
    blk = lax.broadcasted_iota(jnp.int32, sc.shape, 1)
    cnt = jnp.zeros(sc.shape, jnp.int32)
    for i in range(n_sel):
        si = sc[:, i:i + 1]
        ahead = (si > sc) | ((si == sc) & (blk > i))
        cnt = cnt + ahead.astype(jnp.int32)
    return ((cnt < n_top) & (blk < n_sel)).astype(F32)


def _softmax_rows(lg):
    m = jnp.max(lg, axis=-1, keepdims=True)
    e = jnp.exp(lg - m)
    s = jnp.sum(e, axis=-1, keepdims=True)
    return jnp.where(m > 0.5 * NEG, e / s, 0.0)


DEC_SEQ_BLOCK = 2


def _decode_kernel(pt_ref, cache_ref, q_ref, kvn_ref, winn_ref, winbuf_ref, gate_ref,
                   w1_ref, b1_ref, w2_ref, b2_ref, bcmp_ref, bsel_ref, bwin_ref, ovl_ref, esel_ref, _,
                   o_ref, newwin_ref, buf, wbuf, cbuf, sem, *, l, n_pages, page, ls, past, n_sel, n_top):
    s = pl.program_id(0)
    n_steps = pl.num_programs(0)
    slot = s % 2
    nsq = DEC_SEQ_BLOCK
    t_cols = buf.shape[3]
    w_cols = wbuf.shape[2]
    w_len = winbuf_ref.shape[2]
    gw = NSA_GROUPS * NSA_DH

    def page_copy(step, sl, j, pg):
        return pltpu.make_async_copy(cache_ref.at[l, pt_ref[step * nsq + j, pg]],
                                     buf.at[sl, j, :, pl.ds(pg * page, page)], sem.at[sl])

    def start_all(step, sl):
        for j in range(nsq):
            for pg in range(n_pages):
                page_copy(step, sl, j, pg).start()

    @pl.when(s == 0)
    def _():
        start_all(0, 0)
        for sl in range(2):
            for j in range(nsq):
                buf[sl, j, :, pl.ds(past, t_cols - past)] = jnp.zeros((buf.shape[2], t_cols - past), F32)
        for j in range(nsq):
            wbuf[j, :, pl.ds(w_len, w_cols - w_len)] = jnp.zeros((wbuf.shape[1], w_cols - w_len), F32)

    @pl.when(s + 1 < n_steps)
    def _():
        start_all(s + 1, 1 - slot)

    for j in range(nsq):
        for pg in range(n_pages):
            page_copy(s, slot, j, pg).wait()

    n_sub = past // CMP_STRIDE
    half = NSA_GROUPS * CMP_HID
    units = [(j, g) for j in range(nsq) for g in range(NSA_GROUPS)]
    xb = [buf.at[slot, j] for j in range(nsq)]
    for j in range(nsq):
        xb[j][:, pl.ds(past, ls)] = kvn_ref[j]
        wbuf[j, :, pl.ds(0, w_len)] = winbuf_ref[j]
        wbuf[j, :, pl.ds(w_len, ls)] = winn_ref[j]
        newwin_ref[j] = pltpu.roll(wbuf[j], w_cols - ls, 1)[:, 0:w_len]
        for k in range(2):
            cbuf[j, k] = xb[j][k * gw:(k + 1) * gw, pl.ds(0, past)].T

    jk = [(j, k) for j in range(nsq) for k in range(2)]
    xs = [jnp.concatenate([cbuf[j, k, pl.ds(t, n_sub, stride=CMP_STRIDE), :].astype(BF16)
                           for t in range(CMP_STRIDE)], axis=1) for j, k in jk]
    hrs = [_dot(x, w1_ref[k]) for x, (j, k) in zip(xs, jk)]
    hs = [hr[:, :half] + pltpu.roll(hr[:, half:], n_sub - 1, 0) + b1_ref[k] for hr, (j, k) in zip(hrs, jk)]
    summ = [_dot(_silu(h).astype(BF16), w2_ref[k]) + b2_ref[k] for h, (j, k) in zip(hs, jk)]
    ck = {j: summ[2 * j] for j in range(nsq)}
    cv = {j: summ[2 * j + 1] for j in range(nsq)}

    tokpos = past + lax.broadcasted_iota(jnp.int32, (ls, 128), 0)
    blk = lax.broadcasted_iota(jnp.int32, (ls, 128), 1)
    cur = tokpos // SEL_LEN
    forced = (blk == 0) | (blk == cur) | (blk == cur - 1)
    sl_g = lambda g: slice(g * NSA_DH, (g + 1) * NSA_DH)
    qg = {u: (q_ref[u[0], u[1]] * NSA_SCALE).astype(BF16) for u in units}

    lg = {u: _dot_nt(qg[u], ck[u[0]][:, sl_g(u[1])].astype(BF16)) for u in units}
    pb = {u: _softmax_rows(lg[u] + bcmp_ref[u[1]]).astype(BF16) for u in units}
    o_cmp = {u: _dot(pb[u], cv[u[0]][:, sl_g(u[1])].astype(BF16)) for u in units}
    ps = {u: _dot(pb[u], ovl_ref[...]) for u in units}
    mem = {}
    for u in units:
        score = ps[u][0:ls]
        for r in range(1, NSA_HPG):
            score = score + ps[u][r * ls:(r + 1) * ls]
        sc = jnp.where(forced, jnp.inf, jnp.where(blk <= cur, score, -jnp.inf))
        member = _topk_member_lanes(sc, n_sel, n_top)
        mem[u] = jnp.concatenate([member] * NSA_HPG, axis=0).astype(BF16)
    addmask = {u: (_dot(mem[u], esel_ref[...]) - 1.0) * (-NEG) for u in units}

    lg_s = {u: _dot(qg[u], xb[u[0]][2 * gw + u[1] * NSA_DH:2 * gw + (u[1] + 1) * NSA_DH, :].astype(BF16)) for u in units}
    lg_w = {u: _dot(qg[u], wbuf[u[0], u[1] * NSA_DH:(u[1] + 1) * NSA_DH, :].astype(BF16)) for u in units}
    p_s = {u: _softmax_rows(lg_s[u] + bsel_ref[u[1]] + addmask[u]).astype(BF16) for u in units}
    p_w = {u: _softmax_rows(lg_w[u] + bwin_ref[u[1]]).astype(BF16) for u in units}
    o_sel = {u: _dot_nt(p_s[u], xb[u[0]][3 * gw + u[1] * NSA_DH:3 * gw + (u[1] + 1) * NSA_DH, :].astype(BF16))
             for u in units}
    o_win = {u: _dot_nt(p_w[u], wbuf[u[0], gw + u[1] * NSA_DH:gw + (u[1] + 1) * NSA_DH, :].astype(BF16)) for u in units}
    for u in units:
        gt = jax.nn.sigmoid(gate_ref[u[0], u[1]])
        o_ref[u[0], u[1]] = gt[:, 0:1] * o_cmp[u] + gt[:, 1:2] * o_sel[u] + gt[:, 2:3] * o_win[u]


def _decode_call(cache_t, page_table, q_s, kvn_t, winn_t, winbuf_t, gates, cw, tabs, new_win_t, *, l, past, n_sel):
    n_seq, n_pages = page_table.shape
    page = cache_t.shape[3]
    ls = kvn_t.shape[2]
    rows = NSA_HPG * ls
    w_len = winbuf_t.shape[3]
    t_cols = tabs["sel"].shape[-1]
    w_cols = tabs["win"].shape[-1]
    w1, b1, w2, b2 = cw
    gw = NSA_GROUPS * NSA_DH
    nsq = DEC_SEQ_BLOCK
    assert n_seq % nsq == 0
    n_top = min(SEL_TOP, n_sel)
    kern = functools.partial(_decode_kernel, l=l, n_pages=n_pages, page=page, ls=ls, past=past,
                             n_sel=n_sel, n_top=n_top)
    whole = lambda shape: pl.BlockSpec(shape, lambda s, pt: (0,) * len(shape))
    lay = lambda a: pl.BlockSpec((None,) + a.shape[1:], lambda s, pt: (l, 0, 0, 0))
    grid_spec = pltpu.PrefetchScalarGridSpec(
        num_scalar_prefetch=1,
        grid=(n_seq // nsq,),
        in_specs=[
            pl.BlockSpec(memory_space=pl.ANY),
            pl.BlockSpec((nsq, NSA_GROUPS, rows, NSA_DH), lambda s, pt: (s, 0, 0, 0)),
            pl.BlockSpec((nsq, 4 * gw, ls), lambda s, pt: (s, 0, 0)),
            pl.BlockSpec((nsq, 2 * gw, ls), lambda s, pt: (s, 0, 0)),
            pl.BlockSpec((None, nsq, 2 * gw, w_len), lambda s, pt: (l, s, 0, 0)),
            pl.BlockSpec((nsq, NSA_GROUPS, rows, 3), lambda s, pt: (s, 0, 0, 0)),
            lay(w1), lay(b1), lay(w2), lay(b2),
            whole(tabs["cmp"].shape), whole(tabs["sel"].shape), whole(tabs["win"].shape),
            whole(tabs["ovl"].shape), whole(tabs["esel"].shape),
            pl.BlockSpec(memory_space=pl.ANY),
        ],
        out_specs=[
            pl.BlockSpec((nsq, NSA_GROUPS, rows, NSA_DH), lambda s, pt: (s, 0, 0, 0)),
            pl.BlockSpec((None, nsq, 2 * gw, w_len), lambda s, pt: (l, s, 0, 0)),
        ],
        scratch_shapes=[pltpu.VMEM((2, nsq, 4 * gw, t_cols), F32),
                        pltpu.VMEM((nsq, 2 * gw, w_cols), F32),
                        pltpu.VMEM((nsq, 2, past, gw), F32),
                        pltpu.SemaphoreType.DMA((2,))],
    )
    return pl.pallas_call(
        kern,
        grid_spec=grid_spec,
        out_shape=[jax.ShapeDtypeStruct((n_seq, NSA_GROUPS, rows, NSA_DH), F32),
                   jax.ShapeDtypeStruct(new_win_t.shape, F32)],
        input_output_aliases={16: 1},
        compiler_params=pltpu.CompilerParams(dimension_semantics=("arbitrary",),
                                             vmem_limit_bytes=VMEM_LIMIT_BYTES),
        name="nsa_decode",
    )(page_table, cache_t, q_s, kvn_t, winn_t, winbuf_t, gates, w1, b1, w2, b2,
      tabs["cmp"], tabs["sel"], tabs["win"], tabs["ovl"], tabs["esel"], new_win_t)


def _decode_tables(rel_table, *, past, ls, n_cmp, n_sel, w_len):
    rows = NSA_HPG * ls
    t_cols = -(-(n_sel * SEL_LEN) // 128) * 128
    w_cols = -(-(w_len + ls) // 128) * 128
    n_sub = past // CMP_STRIDE
    tok = np.tile(np.arange(ls), NSA_HPG)
    hd = np.repeat(np.arange(NSA_HPG), ls)
    a = np.concatenate([past + tok, past + tok])
    head = np.concatenate([hd, NSA_HPG + hd])
    bc = np.where(np.arange(n_sub) < n_cmp, np.arange(n_sub) * CMP_STRIDE + CMP_LEN - 1, BIG)
    tabs = {
        "cmp": _bias_table(rel_table, a, bc, head, BIG).reshape(NSA_GROUPS, rows, n_sub),
        "sel": _bias_table(rel_table, a, np.arange(t_cols), head, BIG).reshape(NSA_GROUPS, rows, t_cols),
        "win": _bias_table(rel_table, a, past - w_len + np.arange(w_cols), head, WINDOW
                           ).reshape(NSA_GROUPS, rows, w_cols),
    }
    ovl = np.zeros((n_sub, 128), np.float32)
    ovl[:n_cmp, :n_sel] = _block_overlap(n_cmp, n_sel)
    esel = np.zeros((128, t_cols), np.float32)
    esel[np.arange(n_sel * SEL_LEN) // SEL_LEN, np.arange(n_sel * SEL_LEN)] = 1.0
    tabs["ovl"] = jnp.asarray(ovl, BF16)
    tabs["esel"] = jnp.asarray(esel, BF16)
    return tabs


TQ = 128
TKS = 256
SEL_OFFSETS = 3
WIN_OFFSETS = WINDOW // TQ + 2
AUG_ROWS = 32
V_ROWS = NSA_DH + 16


def _prompt_tables(rel_table, *, ln):
    nq = ln // TQ
    n_sub = ln // CMP_STRIDE
    n_cmp = (ln - CMP_LEN) // CMP_STRIDE + 1
    n_sel = -(-ln // SEL_LEN)
    qq = np.tile(np.arange(TQ), NSA_HPG)
    hd = np.repeat(np.arange(NSA_HPG), TQ)
    rows = NSA_HPG * TQ
    shift = TQ // CMP_STRIDE
    width = -(-(n_sub + (nq - 1) * shift) // 256) * 256
    cmp_g = []
    for g in range(NSA_GROUPS):
        master = _bias_table_cols(rel_table, -CMP_STRIDE * (np.arange(width) - (nq - 1) * shift),
                                  (CMP_LEN - 1) - qq, NSA_HPG * g + hd, jnp.zeros((rows,), F32), BIG)
        tiles = jnp.stack([master[(nq - 1 - i) * shift:(nq - 1 - i) * shift + n_sub] for i in range(nq)])
        cmp_g.append(jnp.where((np.arange(n_sub) < n_cmp)[:, None], tiles, NEG))
    cmp_t = jnp.stack(cmp_g, axis=1)
    far = rel_table[REL_BUCKETS - 1]
    far_hi = far.astype(BF16)
    far_lo = (far - far_hi.astype(F32)).astype(BF16)
    far_eff = far_hi.astype(F32) + far_lo.astype(F32)
    sel_t, win_t, crows = [], [], []
    for g in range(NSA_GROUPS):
        head = NSA_HPG * g + hd
        a = np.concatenate([TQ * w - np.arange(TKS) for w in range(WIN_OFFSETS)])
        sub = jnp.take(far_eff, jnp.asarray(head, jnp.int32))
        near = _bias_table_cols(rel_table, a[:SEL_OFFSETS * TKS], -qq, head, sub, BIG).reshape(SEL_OFFSETS, TKS, rows)
        sel_t.append(jnp.concatenate([near, jnp.zeros((1, TKS, rows), F32)], axis=0))
        win_t.append(_bias_table_cols(rel_table, a, -qq, head, jnp.zeros((rows,), F32), WINDOW
                                      ).reshape(WIN_OFFSETS, TKS, rows))
        cr = jnp.zeros((AUG_ROWS, rows), F32)
        cr = cr.at[0].set(jnp.take(far_hi.astype(F32), jnp.asarray(head, jnp.int32)))
        cr = cr.at[1].set(jnp.take(far_lo.astype(F32), jnp.asarray(head, jnp.int32)))
        crows.append(cr)
    nb = -(-n_sel // 8) * 8
    assert nb <= AUG_ROWS
    ovl_t = np.zeros((nb, n_sub), np.float32)
    ovl_t[:n_sel, :n_cmp] = _block_overlap(n_cmp, n_sel).T
    return {"cmp": cmp_t, "sel": jnp.stack(sel_t), "win": jnp.stack(win_t), "crows": jnp.stack(crows),
            "ovl_t": jnp.asarray(ovl_t, BF16), "n_sel": n_sel}


def _pcompress_kernel(k_ref, v_ref, w1_ref, b1_ref, w2_ref, b2_ref, ck_ref, cvt_ref, *, n_sub):
    ck, cv = [_compress_rows(lambda j, src=src: src[pl.ds(j, n_sub, stride=CMP_STRIDE), :],
                             n_sub, w1_ref[k], b1_ref[k], w2_ref[k], b2_ref[k])
              for k, src in ((0, k_ref), (1, v_ref))]
    ck_ref[...] = ck
    cvt_ref[...] = cv.T


def _pcompress_call(p, cw, *, l, nb, ln):
    w1, b1, w2, b2 = cw
    n_sub = ln // CMP_STRIDE
    gw = NSA_GROUPS * NSA_DH
    lay = lambda a: pl.BlockSpec((None,) + a.shape[1:], lambda b: (l, 0, 0, 0))
    out = pl.BlockSpec((None, n_sub, gw), lambda b: (b, 0, 0))
    return pl.pallas_call(
        functools.partial(_pcompress_kernel, n_sub=n_sub),
        grid=(nb,),
        in_specs=[pl.BlockSpec((ln, gw), lambda b: (b, P_KV4 // gw)),
                  pl.BlockSpec((ln, gw), lambda b: (b, P_KV4 // gw + 1)),
                  lay(w1), lay(b1), lay(w2), lay(b2)],
        out_specs=[out, pl.BlockSpec((None, gw, n_sub), lambda b: (b, 0, 0))],
        out_shape=[jax.ShapeDtypeStruct((nb, n_sub, gw), F32), jax.ShapeDtypeStruct((nb, gw, n_sub), F32)],
        compiler_params=pltpu.CompilerParams(dimension_semantics=("parallel",),
                                             vmem_limit_bytes=VMEM_LIMIT_BYTES),
        name="nsa_compress",
    )(p, p, w1, b1, w2, b2)


def _group_queries_t(q_t, g):
    hs = [q_t[(NSA_HPG * g + r) * NSA_DH:(NSA_HPG * g + r + 1) * NSA_DH, :] for r in range(NSA_HPG)]
    return jnp.concatenate(hs, axis=1).astype(F32) * NSA_SCALE


def _cmp_branch(i, q_t, gates_t, ck_ref, cvt_ref, bias_ref, ovl_ref, n_sel, n_top):
    nbk = ovl_ref.shape[0]
    blk = lax.broadcasted_iota(jnp.int32, (nbk, TQ), 0)
    cur = (i * TQ + lax.broadcasted_iota(jnp.int32, (nbk, TQ), 1)) // SEL_LEN
    forced = (blk == 0) | (blk == cur) | (blk == cur - 1)
    groups = range(NSA_GROUPS)
    gsl = [slice(g * NSA_DH, (g + 1) * NSA_DH) for g in groups]
    lg = [_dot(ck_ref[:, gsl[g]].astype(BF16), _group_queries_t(q_t, g).astype(BF16)) + bias_ref[g]
          for g in groups]
    pbs = []
    for g in groups:
        m = jnp.max(lg[g], axis=0, keepdims=True)
        e = jnp.exp(lg[g] - m)
        s = jnp.sum(e, axis=0, keepdims=True)
        pbs.append(jnp.where(m > 0.5 * NEG, e / s, 0.0).astype(BF16))
    o_t = [_dot(cvt_ref[gsl[g], :].astype(BF16), pbs[g]) for g in groups]
    scores = [[_dot(ovl_ref[...], pbs[g][:, r * TQ:(r + 1) * TQ]) for r in range(NSA_HPG)] for g in groups]
    pieces, members = [], []
    for g in groups:
        score_t = functools.reduce(lambda a, b: a + b, scores[g])
        sc = jnp.where(forced, jnp.inf, jnp.where(blk <= cur, score_t, -jnp.inf))
        member_t = _topk_member_rows(sc, n_sel, n_top)
        if nbk < AUG_ROWS:
            member_t = jnp.concatenate([member_t, jnp.zeros((AUG_ROWS - nbk, TQ), F32)], axis=0)
        members.append(member_t)
        for r in range(NSA_HPG):
            h = NSA_HPG * g + r
            pieces.append(gates_t[h:h + 1] * o_t[g][:, r * TQ:(r + 1) * TQ])
    return pieces, members


def _tile_state(lg_t, v_aug_t, valid):
    m = jnp.max(lg_t, axis=0, keepdims=True)
    acc = _dot(v_aug_t, jnp.exp(lg_t - m).astype(BF16))
    if valid is not None:
        m = jnp.where(valid, m, NEG)
    return m, acc


def _merge_states(states):
    m = functools.reduce(jnp.maximum, [s[0] for s in states])
    acc = states[0][1] * jnp.exp(states[0][0] - m)
    for s in states[1:]:
        acc = acc + s[1] * jnp.exp(s[0] - m)
    return m, acc


def _pattn_kernel(q_ref, kv_ref, wkv_ref, wing_ref, ck_ref, cvt_ref, bcmp_ref, ovl_ref, bsel_ref, bwin_ref,
                  crow_ref, _, o_ref, kvt_ref, kk_scr, vs_scr, vw_scr, *, n_sel, n_top):
    i = pl.program_id(1)
    gw = NSA_GROUPS * NSA_DH
    cols = NSA_HPG * TQ
    ln = kv_ref.shape[0]

    @pl.when(i == 0)
    def _():
        pos = lax.broadcasted_iota(jnp.int32, (ln, NSA_DH), 0)
        lane = lax.broadcasted_iota(jnp.int32, (ln, NSA_DH), 1)
        extra = ((lane == pos // SEL_LEN) | (lane == AUG_ROWS) | (lane == AUG_ROWS + 1)).astype(F32)
        ones_row = (lax.broadcasted_iota(jnp.int32, (V_ROWS - NSA_DH, ln), 0) == 0).astype(F32)
        zeros = jnp.zeros((ln, NSA_DH), F32)
        for kind in range(4):
            kvt_ref[kind * gw:(kind + 1) * gw, :] = kv_ref[:, kind * gw:(kind + 1) * gw].T
        vs_t = kvt_ref[3 * gw:4 * gw, :]
        vw_t = wkv_ref[:, gw:2 * gw].T
        for g in range(NSA_GROUPS):
            lo, hi = g * NSA_DH, (g + 1) * NSA_DH
            kk_scr[g] = jnp.concatenate([kv_ref[:, 2 * gw + lo:2 * gw + hi], extra], axis=1).astype(BF16)
            kk_scr[NSA_GROUPS + g] = jnp.concatenate([wkv_ref[:, lo:hi], zeros], axis=1).astype(BF16)
            vs_scr[g] = jnp.concatenate([vs_t[lo:hi], ones_row], axis=0).astype(BF16)
            vw_scr[g] = jnp.concatenate([vw_t[lo:hi], ones_row], axis=0).astype(BF16)

    gates_t = jax.nn.sigmoid(wing_ref[:, 2 * gw:3 * gw]).T
    q_raw = q_ref[...]
    q_t = q_raw.astype(F32) * NSA_SCALE
    m_tile = i // (TKS // TQ)

    def augmented(g, mask_t):
        q_aug = jnp.concatenate(
            [jnp.concatenate([q_t[(NSA_HPG * g + r) * NSA_DH:(NSA_HPG * g + r + 1) * NSA_DH], mask_t], axis=0)
             for r in range(NSA_HPG)], axis=1)
        return jnp.concatenate([q_aug, crow_ref[g]], axis=0).astype(BF16)

    def tile_states(tiles, q_augs):
        lgs = []
        for branch, kt, bias_ref, _ in tiles:
            k0 = pl.multiple_of(kt * TKS, TKS)
            lg = [_dot(kk_scr[branch * NSA_GROUPS + g, pl.ds(k0, TKS), :], q_augs[g]) for g in range(NSA_GROUPS)]
            if bias_ref is not None:
                w = i - kt * (TKS // TQ)
                w = jnp.minimum(w, SEL_OFFSETS) if branch == 0 else w
                lg = [lg[g] + bias_ref[g, w] for g in range(NSA_GROUPS)]
            lgs.append(lg)
        out = []
        for (branch, kt, _, valid), lg in zip(tiles, lgs):
            k0 = pl.multiple_of(kt * TKS, TKS)
            v_scr = vs_scr if branch == 0 else vw_scr
            out.append([_tile_state(lg[g], v_scr[g, :, pl.ds(k0, TKS)], valid) for g in range(NSA_GROUPS)])
        return out

    t_m1, t_m2 = jnp.maximum(m_tile - 1, 0), jnp.maximum(m_tile - 2, 0)
    no_mask = jnp.zeros((AUG_ROWS, TQ), F32)
    wst = tile_states([(1, t_m2, bwin_ref, m_tile >= 2), (1, t_m1, bwin_ref, m_tile >= 1),
                       (1, m_tile, bwin_ref, None)], [augmented(g, no_mask) for g in range(NSA_GROUPS)])
    win = [_merge_states([wst[0][g], wst[1][g], wst[2][g]]) for g in range(NSA_GROUPS)]
    cmp_pieces, members = _cmp_branch(i, q_raw, gates_t, ck_ref, cvt_ref, bcmp_ref, ovl_ref, n_sel, n_top)
    q_augs = [augmented(g, (members[g] - 1.0) * (-NEG)) for g in range(NSA_GROUPS)]

    n_far = jnp.maximum(m_tile - 1, 0)

    def far_body(n, carry):
        t1 = 2 * n + 1
        sts = tile_states([(0, 2 * n, None, None), (0, jnp.minimum(t1, n_far - 1), None, t1 < n_far)], q_augs)
        return tuple(_merge_states([carry[g], sts[0][g], sts[1][g]]) for g in range(NSA_GROUPS))

    init1 = (jnp.full((1, cols), NEG, F32), jnp.zeros((V_ROWS, cols), F32))
    far = lax.fori_loop(0, (n_far + 1) // 2, far_body, (init1,) * NSA_GROUPS)
    sst = tile_states([(0, t_m1, bsel_ref, m_tile >= 1), (0, m_tile, bsel_ref, None)], q_augs)
    sel = [_merge_states([far[g], sst[0][g], sst[1][g]]) for g in range(NSA_GROUPS)]
    pieces = []
    for g in range(NSA_GROUPS):
        acc_s, acc_w = sel[g][1], win[g][1]
        o_s = acc_s[0:NSA_DH] / acc_s[NSA_DH:NSA_DH + 1]
        o_w = acc_w[0:NSA_DH] / acc_w[NSA_DH:NSA_DH + 1]
        for r in range(NSA_HPG):
            h = NSA_HPG * g + r
            cs = slice(r * TQ, (r + 1) * TQ)
            pieces.append(gates_t[NSA_HEADS + h:NSA_HEADS + h + 1] * o_s[:, cs]
                          + gates_t[2 * NSA_HEADS + h:2 * NSA_HEADS + h + 1] * o_w[:, cs])
    o_ref[...] = (jnp.concatenate(cmp_pieces, axis=0) + jnp.concatenate(pieces, axis=0)).T


def _pattn_call(p, q_t, ck, cvt, tabs, kv_acc, *, l, nb, ln):
    nq = ln // TQ
    n_sub = ln // CMP_STRIDE
    n_sel = tabs["n_sel"]
    gw = NSA_GROUPS * NSA_DH
    tile = lambda w, cb: pl.BlockSpec((TQ, w), lambda b, i, cb=cb: (b * nq + i, cb))
    tile_t = pl.BlockSpec((None, NSA_WIDTH, TQ), lambda b, i: (b, 0, i))
    seq = lambda cb: pl.BlockSpec((ln, 4 * gw), lambda b, i, cb=cb: (b, cb))
    whole = lambda a: pl.BlockSpec(a.shape, lambda b, i: (0,) * a.ndim, pipeline_mode=pl.Buffered(1))
    k_scr = pltpu.VMEM((2 * NSA_GROUPS, ln, 2 * NSA_DH), BF16)
    v_scr = pltpu.VMEM((NSA_GROUPS, V_ROWS, ln), BF16)
    return pl.pallas_call(
        functools.partial(_pattn_kernel, n_sel=n_sel, n_top=min(SEL_TOP, n_sel)),
        grid=(nb, nq),
        in_specs=[tile_t, seq(P_KV4 // (4 * gw)), seq(P_WING // (4 * gw)),
                  tile(4 * gw, P_WING // (4 * gw)),
                  pl.BlockSpec((None, n_sub, gw), lambda b, i: (b, 0, 0)),
                  pl.BlockSpec((None, gw, n_sub), lambda b, i: (b, 0, 0)),
                  pl.BlockSpec((None, NSA_GROUPS, n_sub, NSA_HPG * TQ), lambda b, i: (i, 0, 0, 0)),
                  whole(tabs["ovl_t"]),
                  whole(tabs["sel"]), whole(tabs["win"]), whole(tabs["crows"]),
                  pl.BlockSpec(memory_space=pl.ANY)],
        out_specs=[tile(NSA_WIDTH, 0),
                   pl.BlockSpec((None, None, 4 * gw, ln), lambda b, i: (l, b, 0, 0))],
        out_shape=[jax.ShapeDtypeStruct((nb * ln, NSA_WIDTH), F32),
                   jax.ShapeDtypeStruct(kv_acc.shape, F32)],
        input_output_aliases={11: 1},
        scratch_shapes=[k_scr, v_scr, v_scr],
        compiler_params=pltpu.CompilerParams(dimension_semantics=("parallel", "arbitrary"),
                                             vmem_limit_bytes=VMEM_LIMIT_BYTES),
        name="nsa_attention",
    )(q_t, p, p, p, ck, cvt, tabs["cmp"], tabs["ovl_t"], tabs["sel"], tabs["win"], tabs["crows"], kv_acc)


HG_SUB = 8


def _split3(x):
    h = x.astype(BF16)
    r = x - h.astype(F32)
    m = r.astype(BF16)
    return h, m, (r - m.astype(F32)).astype(BF16)


def _row_bcast(x, row, n):
    return jnp.broadcast_to(x[row:row + 1, :], (n, x.shape[1]))


def _hgrn_prompt_kernel(q_ref, f_ref, i_ref, g_ref, lb_ref, gain_ref, tri_ref, ecat_ref, o_ref, s_ref,
                        st_scr, *, n_chunks):
    step = pl.program_id(1)
    c = HG_CHUNK
    n_sub = c // HG_SUB

    @pl.when(step == 0)
    def _():
        st_scr[...] = jnp.zeros_like(st_scr)

    lb = lb_ref[...]
    rowi = lax.broadcasted_iota(jnp.int32, (c, HG_DK), 0)
    rmod = rowi % HG_SUB
    rsub = rowi // HG_SUB
    ti = lax.broadcasted_iota(jnp.int32, (c, c), 0)
    si = lax.broadcasted_iota(jnp.int32, (c, c), 1)
    diag_mask = (ti // HG_SUB == si // HG_SUB)

    for ch in range(n_chunks):
        r0 = ch * c
        fz = f_ref[pl.ds(r0, c), :]
        sg = jax.nn.sigmoid(fz)
        logf = jnp.log(lb + (1.0 - lb) * sg)
        kk = (1.0 - lb) * jax.nn.sigmoid(-fz)
        qq = _silu(q_ref[pl.ds(r0, c), :])
        vv = i_ref[pl.ds(r0, c), :]
        gg = g_ref[pl.ds(r0, c), :]
        h0, h1, h2 = _split3(logf)
        tri = tri_ref[...]
        b_all = _dot(tri, h0) + _dot(tri, h1) + _dot(tri, h2)
        heads = range(HG_HEADS)
        hsl = [slice(h * HG_DK, (h + 1) * HG_DK) for h in heads]
        bs_ = [b_all[:, s] for s in hsl]
        qs = [qq[:, s] for s in hsl]
        ks_ = [kk[:, s] for s in hsl]
        vbs = [vv[:, s].astype(BF16) for s in hsl]
        sts = [st_scr[h] for h in heads]
        qes, lhss, rhss, zcats, kdecs, elast = [], [], [], [], [], []
        for h in heads:
            b, q, k = bs_[h], qs[h], ks_[h]
            qes.append((q * jnp.exp(b)).astype(BF16))
            bstart = jnp.concatenate(
                [jnp.zeros((HG_SUB, HG_DK), F32)]
                + [_row_bcast(b, HG_SUB * i - 1, HG_SUB) for i in range(1, n_sub)], axis=0)
            bend = jnp.concatenate([_row_bcast(b, HG_SUB * (j + 1) - 1, HG_SUB) for j in range(n_sub)], axis=0)
            qh = q * jnp.exp(b - bstart)
            kh = k * jnp.exp(bend - b)
            lhs, rhs = [], []
            for j in range(n_sub - 1):
                bj = b[HG_SUB * (j + 1) - 1:HG_SUB * (j + 1), :]
                dj = jnp.exp(jnp.minimum(bstart - bj, 0.0))
                lhs.append(jnp.where(rsub > j, qh * dj, 0.0).astype(BF16))
                rhs.append(jnp.where(rsub == j, kh, 0.0).astype(BF16))
            lhss.append(jnp.concatenate(lhs, axis=1))
            rhss.append(jnp.concatenate(rhs, axis=1))
            zs = []
            for s in range(HG_SUB):
                ksb = jnp.concatenate([_row_bcast(k, HG_SUB * i + s, HG_SUB) for i in range(n_sub)], axis=0)
                bsb = jnp.concatenate([_row_bcast(b, HG_SUB * i + s, HG_SUB) for i in range(n_sub)], axis=0)
                dec = jnp.where(rmod >= s, jnp.exp(b - bsb), 0.0)
                zs.append((q * ksb * dec).astype(BF16))
            zcats.append(jnp.concatenate(zs, axis=1))
            blast = b[c - 1:c, :]
            kdecs.append((k * jnp.exp(blast - b)).astype(BF16))
            elast.append(jnp.exp(blast))
        o_in = [_dot_nt(qes[h], sts[h].astype(BF16)) for h in heads]
        a_off = [_dot_nt(lhss[h], rhss[h]) for h in heads]
        a_diag = [_dot(zcats[h], ecat_ref[...]) for h in heads]
        upd = [lax.dot_general(vbs[h], kdecs[h], (((0,), (0,)), ((), ())), preferred_element_type=F32)
               for h in heads]
        for h in heads:
            st_scr[h] = sts[h] * elast[h] + upd[h]
        a = [(a_off[h] + jnp.where(diag_mask, a_diag[h], 0.0)).astype(BF16) for h in heads]
        o_intra = [_dot(a[h], vbs[h]) for h in heads]
        outs = []
        for h in heads:
            o = o_in[h] + o_intra[h]
            y = o * lax.rsqrt(jnp.mean(o * o, axis=-1, keepdims=True) + EPS) * gain_ref[...]
            outs.append(y * _silu(gg[:, hsl[h]]))
        o_ref[pl.ds(r0, c), :] = jnp.concatenate(outs, axis=1)

    @pl.when(step == pl.num_programs(1) - 1)
    def _():
        for h in range(HG_HEADS):
            s_ref[h] = st_scr[h].T


def _hgrn_prompt_call(p, lb_l, gain_l, *, nb, ln, tr):
    c = HG_CHUNK
    w = HG_HEADS * HG_DK
    steps = ln // tr
    tri = jnp.asarray(np.tril(np.ones((c, c), np.float32)), BF16)
    ecat = np.zeros((HG_SUB * HG_DK, c), np.float32)
    for s in range(HG_SUB):
        ecat[s * HG_DK:(s + 1) * HG_DK, s::HG_SUB] = 1.0
    col = lambda off: pl.BlockSpec((tr, w), lambda b, i, off=off: (b * steps + i, off // w))
    return pl.pallas_call(
        functools.partial(_hgrn_prompt_kernel, n_chunks=tr // c),
        grid=(nb, steps),
        in_specs=[col(P_QB), col(P_FB), col(P_IB), col(P_GB),
                  pl.BlockSpec((1, w), lambda b, i: (0, 0)),
                  pl.BlockSpec((1, HG_DV), lambda b, i: (0, 0)),
                  pl.BlockSpec((c, c), lambda b, i: (0, 0)),
                  pl.BlockSpec((HG_SUB * HG_DK, c), lambda b, i: (0, 0))],
        out_specs=[pl.BlockSpec((tr, w), lambda b, i: (b * steps + i, 0)),
                   pl.BlockSpec((None, HG_HEADS, HG_DK, HG_DV), lambda b, i: (b, 0, 0, 0))],
        out_shape=[jax.ShapeDtypeStruct((nb * ln, w), F32),
                   jax.ShapeDtypeStruct((nb, HG_HEADS, HG_DK, HG_DV), F32)],
        scratch_shapes=[pltpu.VMEM((HG_HEADS, HG_DV, HG_DK), F32)],
        compiler_params=pltpu.CompilerParams(dimension_semantics=("parallel", "arbitrary"),
                                             vmem_limit_bytes=VMEM_LIMIT_BYTES),
        name="hgrn_prompt",
    )(p, p, p, p, lb_l.reshape(1, w), gain_l.reshape(1, HG_DV), tri, jnp.asarray(ecat, BF16))


HG_SEQ_BLOCK = 4


def _hgrn_decode_kernel(q_ref, f_ref, i_ref, g_ref, lb_ref, gain_ref, s0_ref, _, o_ref, s_ref, *, ls):
    lb = lb_ref[...]
    rowi = lax.broadcasted_iota(jnp.int32, (8, HG_DK), 0)
    zpad = jnp.zeros((8 - ls, HG_DK), F32)
    for n in range(q_ref.shape[0]):
        fz = f_ref[n]
        logf = jnp.log(lb + (1.0 - lb) * jax.nn.sigmoid(fz))
        kk = (1.0 - lb) * jax.nn.sigmoid(-fz)
        qq = _silu(q_ref[n])
        vv = i_ref[n]
        gg = g_ref[n]
        brows = [logf[0:1]]
        for t in range(1, ls):
            brows.append(brows[-1] + logf[t:t + 1])
        b_all = jnp.concatenate(brows, axis=0)
        outs = []
        for h in range(HG_HEADS):
            sl = slice(h * HG_DK, (h + 1) * HG_DK)
            b, q, k, v = b_all[:, sl], qq[:, sl], kk[:, sl], vv[:, sl]
            st = s0_ref[n, h]
            blast = b[ls - 1:ls]
            e_last = jnp.exp(blast)
            e_hi = e_last.astype(BF16)
            e_lo = (e_last - e_hi.astype(F32)).astype(BF16)
            lhs = jnp.concatenate([(k * jnp.exp(blast - b)).astype(BF16), e_hi, e_lo,
                                   jnp.zeros((8 - ls - 2, HG_DK), BF16)], axis=0)
            ones2 = ((rowi >= ls) & (rowi < ls + 2)).astype(F32)
            rhs = jnp.concatenate([jnp.concatenate([v, zpad], axis=0), ones2], axis=1).astype(BF16)
            upd = lax.dot_general(lhs, rhs, (((0,), (0,)), ((), ())), preferred_element_type=F32)
            s_ref[n, h] = upd[:, HG_DV:] * st + upd[:, :HG_DV]
            o = _dot(jnp.concatenate([q * jnp.exp(b), zpad], axis=0).astype(BF16), st.astype(BF16))[0:ls]
            for s in range(ls):
                keep = rowi[0:ls] >= s
                dec = jnp.where(keep, jnp.exp(jnp.minimum(b - b[s:s + 1], 0.0)), 0.0)
                a_s = jnp.sum(q * k[s:s + 1] * dec, axis=-1, keepdims=True)
                o = o + a_s * v[s:s + 1]
            y = o * lax.rsqrt(jnp.mean(o * o, axis=-1, keepdims=True) + EPS) * gain_ref[...]
            outs.append(y * _silu(gg[:, sl]))
        o_ref[n] = jnp.concatenate(outs, axis=1)


def _hgrn_decode_call(qb, fb, ib, gb, lb_l, gain_l, state, new_state, *, l):
    n_seq, ls, w = qb.shape
    sb = HG_SEQ_BLOCK
    assert ls + 2 <= 8 and n_seq % sb == 0
    tok = pl.BlockSpec((sb, ls, w), lambda s: (s, 0, 0))
    layer = pl.BlockSpec((None, sb, HG_HEADS, HG_DK, HG_DV), lambda s: (l, s, 0, 0, 0))
    return pl.pallas_call(
        functools.partial(_hgrn_decode_kernel, ls=ls),
        grid=(n_seq // sb,),
        in_specs=[tok, tok, tok, tok,
                  pl.BlockSpec((1, w), lambda s: (0, 0)),
                  pl.BlockSpec((1, HG_DV), lambda s: (0, 0)),
                  layer, pl.BlockSpec(memory_space=pl.ANY)],
        out_specs=[tok, layer],
        out_shape=[jax.ShapeDtypeStruct((n_seq, ls, w), F32),
                   jax.ShapeDtypeStruct(new_state.shape, F32)],
        input_output_aliases={7: 1},
        compiler_params=pltpu.CompilerParams(dimension_semantics=("parallel",)),
        name="hgrn_decode",
    )(qb, fb, ib, gb, lb_l.reshape(1, w), gain_l.reshape(1, HG_DV), state, new_state)


def kernel(x_prompt, x_sample, cache_nsa_kv, state_win_kv, state_hgrn, page_table, c_prompt, c_sample,
           norm_g, ada_w, ada_b, ffn_w_in, ffn_w_out, w_in, cmp_w1, cmp_b1, cmp_w2, cmp_b2, rel_table,
           hgrn_lb_logits, hgrn_norm_g, w_branch_nsa, w_branch_hgrn, w_out, final_g):
    depth = norm_g.shape[0]
    bp, lp, d = x_prompt.shape
    bs, ls, _ = x_sample.shape
    n_seq, n_pages = page_table.shape
    n_pool, page = cache_nsa_kv.shape[1:3]
    past = n_pages * page
    w_len = state_win_kv.shape[2]
    gw = NSA_GROUPS * NSA_DH
    assert bs == 128 and n_seq == bs, "sample rows are tiled per 128 sequences"
    assert lp % TKS == 0 and lp >= WINDOW and (lp // CMP_STRIDE) % 128 == 0
    n_cmp_s = (past + ls - CMP_LEN) // CMP_STRIDE + 1
    n_sel_s = -(-(past + ls) // SEL_LEN)
    assert (n_cmp_s - 1) * CMP_STRIDE + CMP_LEN <= past and (past // CMP_STRIDE) % 128 == 0
    assert past % SEL_LEN == 0 and ls <= 8 and past // SEL_LEN == (past + ls - 1) // SEL_LEN

    lb = jnp.cumsum(jax.nn.softmax(hgrn_lb_logits.astype(F32), axis=0), axis=0)
    lb = lb - lb[:1]

    w_in_p, wq_t, w_gate = _split_w_in(w_in)
    wpa_b = w_branch_nsa.astype(BF16)
    wpb_b = w_branch_hgrn.astype(BF16)
    wo_b = w_out.astype(BF16)
    cw = _compress_weights(cmp_w1, cmp_b1, cmp_w2, cmp_b2)
    tabs_p = _prompt_tables(rel_table, ln=lp)
    tabs_s = _decode_tables(rel_table, past=past, ls=ls, n_cmp=n_cmp_s, n_sel=n_sel_s, w_len=w_len)
    cache_t = cache_nsa_kv.reshape(depth, n_pool, page, 4 * gw).transpose(0, 1, 3, 2)
    winbuf_t = state_win_kv.reshape(depth, bs, w_len, 2 * gw).transpose(0, 1, 3, 2)

    mods = _mods_call(jnp.concatenate([c_sample, c_prompt], axis=0), ada_w, ada_b)

    def sublayers(x2, l, tm, seq_tiles, mixers):
        x2 = _ffn_call(x2, norm_g, mods, ffn_w_in, ffn_w_out, final_g, l=l, s=0, sub=0, tm=tm, tf=256,
                       seq_tiles=seq_tiles, final=False)
        tm_proj = tm if seq_tiles is None else tm * seq_tiles[1]
        p, q_t = _proj_call(x2, norm_g, mods, w_in_p, wq_t, l=l, tm=tm_proj, tn=512,
                            seq_tiles=None if seq_tiles is None else (seq_tiles[0], 1))
        o_a, o_b, extras = mixers(p, q_t)
        x2 = _merge_call(x2, o_a, o_b, norm_g, mods, w_gate, wpa_b, wpb_b, wo_b, l=l, tm=tm, seq_tiles=seq_tiles)
        x2 = _ffn_call(x2, norm_g, mods, ffn_w_in, ffn_w_out, final_g, l=l, s=1, sub=2, tm=tm, tf=256,
                       seq_tiles=seq_tiles, final=(l == depth - 1))
        return x2, extras

    tm_p = 1024 if lp % 1024 == 0 else lp
    x2 = x_prompt.reshape(bp * lp, d)
    win_p, hg_p = [], []
    kv_acc = jnp.zeros((depth, bp, 4 * gw, lp), F32)
    for l in range(depth):
        def prompt_mixers(p, q_t, kv_acc, l=l):
            ck, cvt = _pcompress_call(p, cw, l=l, nb=bp, ln=lp)
            o_a, kv_acc = _pattn_call(p, q_t, ck, cvt, tabs_p, kv_acc, l=l, nb=bp, ln=lp)
            o_b, s_new = _hgrn_prompt_call(p, lb[l], hgrn_norm_g[l], nb=bp, ln=lp, tr=256)
            p3 = p.reshape(bp, lp, P_TOTAL)
            nwin = p3[:, lp - min(WINDOW, lp):, P_WING:P_WING + 2 * gw].reshape(bp, -1, 2, NSA_GROUPS, NSA_DH)
            return o_a, o_b, (kv_acc, nwin, s_new)

        x2, (kv_acc, nwin, s_new) = sublayers(x2, l, tm_p, (bs, lp // tm_p),
                                              functools.partial(prompt_mixers, kv_acc=kv_acc))
        win_p.append(nwin)
        hg_p.append(s_new)
    y_p = x2.reshape(bp, lp, d)
    kv_p = kv_acc.transpose(0, 1, 3, 2).reshape(depth, bp, lp, 4, NSA_GROUPS, NSA_DH)

    x2 = x_sample.transpose(1, 0, 2).reshape(ls * bs, d)
    kv_s = []
    win_acc = jnp.zeros(winbuf_t.shape, F32)
    hg_acc = jnp.zeros(state_hgrn.shape, F32)
    for l in range(depth):
        def sample_mixers(p, q_t, win_acc, hg_acc, l=l):
            ps = p.reshape(ls, bs, P_TOTAL).transpose(1, 0, 2)
            q_s = q_t.astype(F32).reshape(NSA_GROUPS, NSA_HPG, NSA_DH, ls, bs)
            q_s = q_s.transpose(4, 0, 1, 3, 2).reshape(bs, NSA_GROUPS, NSA_HPG * ls, NSA_DH)
            ga = ps[:, :, P_WING + 2 * gw:P_WING + 2 * gw + 3 * NSA_HEADS].reshape(bs, ls, 3, NSA_GROUPS, NSA_HPG)
            ga = ga.transpose(0, 3, 4, 1, 2).reshape(bs, NSA_GROUPS, NSA_HPG * ls, 3)
            kvn = ps[:, :, P_KV4:P_KV4 + 4 * gw]
            winn = ps[:, :, P_WING:P_WING + 2 * gw]
            o, win_acc = _decode_call(cache_t, page_table, q_s, kvn.transpose(0, 2, 1), winn.transpose(0, 2, 1),
                                      winbuf_t, ga, cw, tabs_s, win_acc, l=l, past=past, n_sel=n_sel_s)
            o_a = o.reshape(bs, NSA_GROUPS, NSA_HPG, ls, NSA_DH).transpose(3, 0, 1, 2, 4).reshape(ls * bs, NSA_WIDTH)
            o_b, hg_acc = _hgrn_decode_call(ps[:, :, P_QB:P_QB + 512], ps[:, :, P_FB:P_FB + 512],
                                            ps[:, :, P_IB:P_IB + 512], ps[:, :, P_GB:P_GB + 512],
                                            lb[l], hgrn_norm_g[l], state_hgrn, hg_acc, l=l)
            o_b = o_b.transpose(1, 0, 2).reshape(ls * bs, HG_WIDTH)
            return o_a, o_b, (kvn.reshape(bs, ls, 4, NSA_GROUPS, NSA_DH), win_acc, hg_acc)

        x2, (kv4, win_acc, hg_acc) = sublayers(
            x2, l, ls * bs, None, functools.partial(sample_mixers, win_acc=win_acc, hg_acc=hg_acc))
        kv_s.append(kv4)
    y_s = x2.reshape(ls, bs, d).transpose(1, 0, 2)
    win_s = win_acc.transpose(0, 1, 3, 2).reshape(depth, bs, w_len, 2, NSA_GROUPS, NSA_DH)

    return (y_p, y_s, kv_p, jnp.stack(kv_s), jnp.stack(win_p), win_s, jnp.stack(hg_p), hg_acc)
```

```python
import functools
import math

import jax
import jax.numpy as jnp
import numpy as np
from jax import lax
from jax.experimental import pallas as pl
from jax.experimental.pallas import tpu as pltpu

NSA_HEADS = 8
NSA_GROUPS = 2
NSA_HPG = NSA_HEADS // NSA_GROUPS
NSA_DH = 64
NSA_WIDTH = NSA_HEADS * NSA_DH
NSA_SCALE = NSA_DH ** -0.5
CMP_LEN = 32
CMP_STRIDE = 16
CMP_RATIO = CMP_LEN // CMP_STRIDE
CMP_HID = 128
SEL_LEN = 64
SEL_TOP = 16
WINDOW = 512
HG_HEADS = 4
HG_DK = 128
HG_DV = 128
HG_WIDTH = HG_HEADS * HG_DV
HG_CHUNK = 64
REL_BUCKETS = 32
REL_MAX_DIST = 128
EPS = 1e-6

BF16 = jnp.bfloat16
F32 = jnp.float32

VMEM_LIMIT_BYTES = 56 * 1024 * 1024

P_KV4, P_WING, P_QB, P_FB, P_IB, P_GB, P_TOTAL = (0, 512, 1024, 1536, 2048, 2560, 3072)


def _split_w_in(w_in):
    d_model = w_in.shape[1]
    sizes = (NSA_WIDTH, 6 * NSA_GROUPS * NSA_DH, 3 * NSA_HEADS, HG_HEADS * HG_DK, HG_HEADS * HG_DK,
             HG_WIDTH, HG_WIDTH, d_model, d_model)
    offs = np.concatenate([[0], np.cumsum(sizes)])
    qa, kva, ga, qb, fb, ib, gb, ma, mb = [w_in[:, :, int(offs[i]):int(offs[i + 1])] for i in range(9)]
    pad = jnp.zeros(w_in.shape[:2] + (P_QB - P_KV4 - kva.shape[2] - ga.shape[2],), w_in.dtype)
    out = jnp.concatenate([kva, ga, pad, qb, fb, ib, gb], axis=2)
    assert out.shape[2] == P_TOTAL
    return (out.astype(BF16), qa.transpose(0, 2, 1).astype(BF16),
            jnp.concatenate([ma, mb], axis=2).astype(BF16))


def _mods_kernel(c_ref, w_ref, b_ref, o_ref):
    c = c_ref[...]
    sc = (c * jax.nn.sigmoid(c)).astype(BF16)
    o_ref[...] = jnp.dot(sc, w_ref[...].astype(BF16), preferred_element_type=F32) + b_ref[...]


def _mods_call(c_all, ada_w, ada_b):
    depth, d, nd = ada_w.shape
    n = c_all.shape[0]
    b3 = ada_b.reshape(depth, 1, nd)
    return pl.pallas_call(
        _mods_kernel,
        grid=(depth, nd // d),
        in_specs=[
            pl.BlockSpec((n, d), lambda l, j: (0, 0)),
            pl.BlockSpec((None, d, d), lambda l, j: (l, 0, j)),
            pl.BlockSpec((None, 1, d), lambda l, j: (l, 0, j)),
        ],
        out_specs=pl.BlockSpec((None, None, n, d), lambda l, j: (l, j, 0, 0)),
        out_shape=jax.ShapeDtypeStruct((depth, nd // d, n, d), F32),
        compiler_params=pltpu.CompilerParams(dimension_semantics=("arbitrary", "arbitrary")),
        name="adaln_mods",
    )(c_all, ada_w, b3)


def _rows(m, tm):
    r = m.shape[0]
    if r == 1 or r == tm:
        return m
    return jnp.concatenate([m] * (tm // r), axis=0)


def _norm_mod(x, g, shift, scale):
    y = x * lax.rsqrt(jnp.mean(x * x, axis=-1, keepdims=True) + EPS) * g
    return y * (1.0 + scale) + shift


def _mod_spec(mods, l, k, tm, seq_tiles):
    if seq_tiles is not None:
        row0, tiles_per_seq = seq_tiles
        d = mods.shape[-1]
        m5 = mods.reshape(mods.shape[0], mods.shape[1], mods.shape[2], 1, d)
        return m5, pl.BlockSpec((None, None, None, 1, d),
                                lambda i, j, l=l, k=k: (l, k, row0 + i // tiles_per_seq, 0, 0))
    d = mods.shape[-1]
    return mods, pl.BlockSpec((None, None, 128, d), lambda i, j, l=l, k=k: (l, k, 0, 0))


def _ffn_kernel(x_ref, g_ref, sh_ref, sc_ref, gt_ref, wa_ref, wb_ref, wo_ref, fg_ref, o_ref,
                n_scr, acc_scr, *, nf, final):
    j = pl.program_id(1)
    tm = x_ref.shape[0]

    @pl.when(j == 0)
    def _():
        n = _norm_mod(x_ref[...], g_ref[...], _rows(sh_ref[...], tm), _rows(sc_ref[...], tm))
        n_scr[...] = n.astype(BF16)
        acc_scr[...] = jnp.zeros_like(acc_scr)

    n = n_scr[...]
    a = jnp.dot(n, wa_ref[...].astype(BF16), preferred_element_type=F32)
    b = jnp.dot(n, wb_ref[...].astype(BF16), preferred_element_type=F32)
    h = (a * jax.nn.sigmoid(a) * b).astype(BF16)
    acc_scr[...] += jnp.dot(h, wo_ref[...].astype(BF16), preferred_element_type=F32)

    @pl.when(j == nf - 1)
    def _():
        y = x_ref[...] + 0.5 * _rows(gt_ref[...], tm) * acc_scr[...]
        if final:
            y = y * lax.rsqrt(jnp.mean(y * y, axis=-1, keepdims=True) + EPS) * fg_ref[...]
        o_ref[...] = y


def _ffn_call(x, norm_g, mods, ffn_w_in, ffn_w_out, final_g, *, l, s, sub, tm, tf, seq_tiles, final):
    t, d = x.shape
    d_ff = ffn_w_out.shape[2]
    nf = d_ff // tf
    g2 = norm_g[l, sub].reshape(1, d)
    fg2 = final_g.reshape(1, d)
    m_sh, sp_sh = _mod_spec(mods, l, 3 * sub, tm, seq_tiles)
    m_sc, sp_sc = _mod_spec(mods, l, 3 * sub + 1, tm, seq_tiles)
    m_gt, sp_gt = _mod_spec(mods, l, 3 * sub + 2, tm, seq_tiles)
    return pl.pallas_call(
        functools.partial(_ffn_kernel, nf=nf, final=final),
        grid=(t // tm, nf),
        in_specs=[
            pl.BlockSpec((tm, d), lambda i, j: (i, 0)),
            pl.BlockSpec((1, d), lambda i, j: (0, 0)),
            sp_sh, sp_sc, sp_gt,
            pl.BlockSpec((None, None, d, tf), lambda i, j: (l, s, 0, j)),
            pl.BlockSpec((None, None, d, tf), lambda i, j: (l, s, 0, nf + j)),
            pl.BlockSpec((None, None, tf, d), lambda i, j: (l, s, j, 0)),
            pl.BlockSpec((1, d), lambda i, j: (0, 0)),
        ],
        out_specs=pl.BlockSpec((tm, d), lambda i, j: (i, 0)),
        out_shape=jax.ShapeDtypeStruct((t, d), F32),
        scratch_shapes=[pltpu.VMEM((tm, d), BF16), pltpu.VMEM((tm, d), F32)],
        compiler_params=pltpu.CompilerParams(
            dimension_semantics=("parallel", "arbitrary"), vmem_limit_bytes=VMEM_LIMIT_BYTES),
        name="ffn",
    )(x, g2, m_sh, m_sc, m_gt, ffn_w_in, ffn_w_in, ffn_w_out, fg2)


def _proj_kernel(x_ref, g_ref, sh_ref, sc_ref, w_ref, wq_ref, o_ref, qt_ref, n_scr):
    j = pl.program_id(1)
    tm = x_ref.shape[0]

    @pl.when(j == 0)
    def _():
        n = _norm_mod(x_ref[...], g_ref[...], _rows(sh_ref[...], tm), _rows(sc_ref[...], tm))
        n_scr[...] = n.astype(BF16)
        qt_ref[...] = _dot_nt(wq_ref[...], n_scr[...]).astype(BF16)

    o_ref[...] = jnp.dot(n_scr[...], w_ref[...], preferred_element_type=F32)


def _proj_call(x, norm_g, mods, w_in_p, wq_t, *, l, tm, tn, seq_tiles):
    t, d = x.shape
    n_out = w_in_p.shape[2]
    g2 = norm_g[l, 1].reshape(1, d)
    m_sh, sp_sh = _mod_spec(mods, l, 3, tm, seq_tiles)
    m_sc, sp_sc = _mod_spec(mods, l, 4, tm, seq_tiles)
    return pl.pallas_call(
        _proj_kernel,
        grid=(t // tm, n_out // tn),
        in_specs=[
            pl.BlockSpec((tm, d), lambda i, j: (i, 0)),
            pl.BlockSpec((1, d), lambda i, j: (0, 0)),
            sp_sh, sp_sc,
            pl.BlockSpec((None, d, tn), lambda i, j: (l, 0, j)),
            pl.BlockSpec((None, NSA_WIDTH, d), lambda i, j: (l, 0, 0)),
        ],
        out_specs=[pl.BlockSpec((tm, tn), lambda i, j: (i, j)),
                   pl.BlockSpec((None, NSA_WIDTH, tm), lambda i, j: (i, 0, 0))],
        out_shape=[jax.ShapeDtypeStruct((t, n_out), F32),
                   jax.ShapeDtypeStruct((t // tm, NSA_WIDTH, tm), BF16)],
        scratch_shapes=[pltpu.VMEM((tm, d), BF16)],
        compiler_params=pltpu.CompilerParams(
            dimension_semantics=("parallel", "arbitrary"), vmem_limit_bytes=VMEM_LIMIT_BYTES),
        name="in_proj",
    )(x, g2, m_sh, m_sc, w_in_p, wq_t)


def _merge_kernel(x_ref, oa_ref, ob_ref, g_ref, sh_ref, sc_ref, gt_ref, wm_ref, wpa_ref, wpb_ref, wo_ref, o_ref):
    tm, d = x_ref.shape
    x = x_ref[...]
    n = _norm_mod(x, g_ref[...], _rows(sh_ref[...], tm), _rows(sc_ref[...], tm)).astype(BF16)
    ma = jnp.dot(n, wm_ref[:, 0:d], preferred_element_type=F32)
    mb = jnp.dot(n, wm_ref[:, d:2 * d], preferred_element_type=F32)
    ya = jnp.dot(oa_ref[...].astype(BF16), wpa_ref[...], preferred_element_type=F32)
    yb = jnp.dot(ob_ref[...].astype(BF16), wpb_ref[...], preferred_element_type=F32)
    merged = jax.nn.sigmoid(ma) * ya + jax.nn.sigmoid(mb) * yb
    y = jnp.dot(merged.astype(BF16), wo_ref[...], preferred_element_type=F32)
    o_ref[...] = x + _rows(gt_ref[...], tm) * y


def _merge_call(x, o_a, o_b, norm_g, mods, w_gate, wpa, wpb, wo, *, l, tm, seq_tiles):
    t, d = x.shape
    g2 = norm_g[l, 1].reshape(1, d)
    m_sh, sp_sh = _mod_spec(mods, l, 3, tm, seq_tiles)
    m_sc, sp_sc = _mod_spec(mods, l, 4, tm, seq_tiles)
    m_gt, sp_gt = _mod_spec(mods, l, 5, tm, seq_tiles)
    one = lambda i, j: (i, 0)
    return pl.pallas_call(
        _merge_kernel,
        grid=(t // tm, 1),
        in_specs=[
            pl.BlockSpec((tm, d), one),
            pl.BlockSpec((tm, NSA_WIDTH), one),
            pl.BlockSpec((tm, HG_WIDTH), one),
            pl.BlockSpec((1, d), lambda i, j: (0, 0)),
            sp_sh, sp_sc, sp_gt,
            pl.BlockSpec((None, d, 2 * d), lambda i, j: (l, 0, 0)),
            pl.BlockSpec((None, NSA_WIDTH, d), lambda i, j: (l, 0, 0)),
            pl.BlockSpec((None, HG_WIDTH, d), lambda i, j: (l, 0, 0)),
            pl.BlockSpec((None, d, d), lambda i, j: (l, 0, 0)),
        ],
        out_specs=pl.BlockSpec((tm, d), one),
        out_shape=jax.ShapeDtypeStruct((t, d), F32),
        compiler_params=pltpu.CompilerParams(
            dimension_semantics=("parallel", "arbitrary"), vmem_limit_bytes=VMEM_LIMIT_BYTES),
        name="merge_out",
    )(x, o_a, o_b, g2, m_sh, m_sc, m_gt, w_gate, wpa, wpb, wo)


NEG = -1e30


def _bucket_thresholds():
    n = np.arange(0, 4 * REL_MAX_DIST)
    exact = REL_BUCKETS // 2
    large = exact + np.floor(np.log(np.maximum(n, 1) / exact) / math.log(REL_MAX_DIST / exact)
                             * (REL_BUCKETS - exact)).astype(np.int64)
    bucket = np.where(n < exact, n, np.minimum(large, REL_BUCKETS - 1))
    return [int(np.min(n[bucket >= k])) for k in range(REL_BUCKETS)]


_THRESHOLDS = _bucket_thresholds()


def _bias_kernel(tab_ref, a_ref, b_ref, o_ref, *, hi):
    dist = a_ref[...] - b_ref[...]
    tab = tab_ref[...]
    v = jnp.broadcast_to(tab[:, REL_BUCKETS - 1:REL_BUCKETS], dist.shape)
    for k in range(REL_BUCKETS - 2, -1, -1):
        v = jnp.where(dist < _THRESHOLDS[k + 1], tab[:, k:k + 1], v)
    o_ref[...] = jnp.where((dist >= 0) & (dist < hi), v, NEG)


def _bias_table(rel_table, a, b, head, hi):
    r, c = a.shape[0], b.shape[0]
    tr = min(r, 512)
    assert r % tr == 0
    tab = jnp.take(rel_table, jnp.asarray(head, jnp.int32), axis=1).T
    return pl.pallas_call(
        functools.partial(_bias_kernel, hi=hi),
        grid=(r // tr,),
        in_specs=[pl.BlockSpec((tr, REL_BUCKETS), lambda i: (i, 0)),
                  pl.BlockSpec((tr, 1), lambda i: (i, 0)),
                  pl.BlockSpec((1, c), lambda i: (0, 0))],
        out_specs=pl.BlockSpec((tr, c), lambda i: (i, 0)),
        out_shape=jax.ShapeDtypeStruct((r, c), F32),
        name="rel_bias_table",
    )(tab, jnp.asarray(a, jnp.int32).reshape(r, 1), jnp.asarray(b, jnp.int32).reshape(1, c))


def _bias_cols_kernel(tab_ref, a_ref, b_ref, sub_ref, o_ref, *, hi):
    dist = a_ref[...] - b_ref[...]
    tab = tab_ref[...]
    v = jnp.broadcast_to(tab[REL_BUCKETS - 1:REL_BUCKETS, :], dist.shape)
    for k in range(REL_BUCKETS - 2, -1, -1):
        v = jnp.where(dist < _THRESHOLDS[k + 1], tab[k:k + 1, :], v)
    o_ref[...] = jnp.where((dist >= 0) & (dist < hi), v - sub_ref[...], NEG)


def _bias_table_cols(rel_table, a, b, head, sub, hi):
    r, c = a.shape[0], b.shape[0]
    tr = min(r, 256)
    assert r % tr == 0
    tab = jnp.take(rel_table, jnp.asarray(head, jnp.int32), axis=1)
    return pl.pallas_call(
        functools.partial(_bias_cols_kernel, hi=hi),
        grid=(r // tr,),
        in_specs=[pl.BlockSpec((REL_BUCKETS, c), lambda i: (0, 0)),
                  pl.BlockSpec((tr, 1), lambda i: (i, 0)),
                  pl.BlockSpec((1, c), lambda i: (0, 0)),
                  pl.BlockSpec((1, c), lambda i: (0, 0))],
        out_specs=pl.BlockSpec((tr, c), lambda i: (i, 0)),
        out_shape=jax.ShapeDtypeStruct((r, c), F32),
        name="rel_bias_table_t",
    )(tab, jnp.asarray(a, jnp.int32).reshape(r, 1), jnp.asarray(b, jnp.int32).reshape(1, c), sub.reshape(1, c))


BIG = 1 << 30


def _dot_nt(a, b):
    return lax.dot_general(a, b, (((1,), (1,)), ((), ())), preferred_element_type=F32)


def _dot(a, b):
    return jnp.dot(a, b, preferred_element_type=F32)


def _silu(x):
    return x * jax.nn.sigmoid(x)


def _block_overlap(n_cmp, n_sel):
    c0 = np.arange(n_cmp) * CMP_STRIDE
    s0 = np.arange(n_sel) * SEL_LEN
    return ((c0[:, None] <= s0[None] + SEL_LEN - 1) & (c0[:, None] + CMP_LEN - 1 >= s0[None])).astype(np.float32)


def _compress_weights(cmp_w1, cmp_b1, cmp_w2, cmp_b2):
    eye = jnp.eye(NSA_GROUPS, dtype=F32)
    w1 = jnp.einsum('lkrjde,gh->lkjgdrhe', cmp_w1, eye)
    depth = cmp_w1.shape[0]
    w1 = w1.reshape(depth, 2, CMP_STRIDE * NSA_GROUPS * NSA_DH, CMP_RATIO * NSA_GROUPS * CMP_HID).astype(BF16)
    w2 = jnp.einsum('lked,gh->lkgehd', cmp_w2, eye).reshape(depth, 2, NSA_GROUPS * CMP_HID, NSA_GROUPS * NSA_DH)
    b1 = jnp.tile(cmp_b1, (1, 1, NSA_GROUPS)).reshape(depth, 2, 1, NSA_GROUPS * CMP_HID)
    b2 = jnp.tile(cmp_b2, (1, 1, NSA_GROUPS)).reshape(depth, 2, 1, NSA_GROUPS * NSA_DH)
    return w1, b1, w2.astype(BF16), b2


def _compress_rows(load_piece, n_sub, w1, b1, w2, b2):
    x = jnp.concatenate([load_piece(j).astype(BF16) for j in range(CMP_STRIDE)], axis=1)
    hr = _dot(x, w1)
    half = NSA_GROUPS * CMP_HID
    h = hr[:, :half] + pltpu.roll(hr[:, half:], n_sub - 1, 0) + b1
    return _dot(_silu(h).astype(BF16), w2) + b2


def _topk_member_rows(sc, n_sel, n_top):
    blk = lax.broadcasted_iota(jnp.int32, sc.shape, 0)
    cnt = jnp.zeros(sc.shape, jnp.int32)
    for i in range(n_sel):
        si = sc[i:i + 1, :]
        ahead = (si > sc) | ((si == sc) & (blk > i))
        cnt = cnt + ahead.astype(jnp.int32)
    return ((cnt < n_top) & (blk < n_sel)).astype(F32)


def _topk_member_lanes(sc, n_sel, n_top):
    blk = lax.broadcasted_iota(jnp.int32, sc.shape, 1)
    cnt = jnp.zeros(sc.shape, jnp.int32)
    for i in range(n_sel):
        si = sc[:, i:i + 1]
        ahead = (si > sc) | ((si == sc) & (blk > i))
        cnt = cnt + ahead.astype(jnp.int32)
    return ((cnt < n_top) & (blk < n_sel)).astype(F32)


def _softmax_rows(lg):
    m = jnp.max(lg, axis=-1, keepdims=True)
    e = jnp.exp(lg - m)
    s = jnp.sum(e, axis=-1, keepdims=True)
    return jnp.where(m > 0.5 * NEG, e / s, 0.0)


DEC_SEQ_BLOCK = 2


def _decode_kernel(pt_ref, cache_ref, q_ref, kvn_ref, winn_ref, winbuf_ref, gate_ref,
                   w1_ref, b1_ref, w2_ref, b2_ref, bcmp_ref, bsel_ref, bwin_ref, ovl_ref, esel_ref, _,
                   o_ref, newwin_ref, buf, wbuf, cbuf, sem, *, l, n_pages, page, ls, past, n_sel, n_top):
    s = pl.program_id(0)
    n_steps = pl.num_programs(0)
    slot = s % 2
    nsq = DEC_SEQ_BLOCK
    t_cols = buf.shape[3]
    w_cols = wbuf.shape[2]
    w_len = winbuf_ref.shape[2]
    gw = NSA_GROUPS * NSA_DH

    def page_copy(step, sl, j, pg):
        return pltpu.make_async_copy(cache_ref.at[l, pt_ref[step * nsq + j, pg]],
                                     buf.at[sl, j, :, pl.ds(pg * page, page)], sem.at[sl])

    def start_all(step, sl):
        for j in range(nsq):
            for pg in range(n_pages):
                page_copy(step, sl, j, pg).start()

    @pl.when(s == 0)
    def _():
        start_all(0, 0)
        for sl in range(2):
            for j in range(nsq):
                buf[sl, j, :, pl.ds(past, t_cols - past)] = jnp.zeros((buf.shape[2], t_cols - past), F32)
        for j in range(nsq):
            wbuf[j, :, pl.ds(w_len, w_cols - w_len)] = jnp.zeros((wbuf.shape[1], w_cols - w_len), F32)

    @pl.when(s + 1 < n_steps)
    def _():
        start_all(s + 1, 1 - slot)

    for j in range(nsq):
        for pg in range(n_pages):
            page_copy(s, slot, j, pg).wait()

    n_sub = past // CMP_STRIDE
    half = NSA_GROUPS * CMP_HID
    units = [(j, g) for j in range(nsq) for g in range(NSA_GROUPS)]
    xb = [buf.at[slot, j] for j in range(nsq)]
    for j in range(nsq):
        xb[j][:, pl.ds(past, ls)] = kvn_ref[j]
        wbuf[j, :, pl.ds(0, w_len)] = winbuf_ref[j]
        wbuf[j, :, pl.ds(w_len, ls)] = winn_ref[j]
        newwin_ref[j] = pltpu.roll(wbuf[j], w_cols - ls, 1)[:, 0:w_len]
        for k in range(2):
            cbuf[j, k] = xb[j][k * gw:(k + 1) * gw, pl.ds(0, past)].T

    jk = [(j, k) for j in range(nsq) for k in range(2)]
    xs = [jnp.concatenate([cbuf[j, k, pl.ds(t, n_sub, stride=CMP_STRIDE), :].astype(BF16)
                           for t in range(CMP_STRIDE)], axis=1) for j, k in jk]
    hrs = [_dot(x, w1_ref[k]) for x, (j, k) in zip(xs, jk)]
    hs = [hr[:, :half] + pltpu.roll(hr[:, half:], n_sub - 1, 0) + b1_ref[k] for hr, (j, k) in zip(hrs, jk)]
    summ = [_dot(_silu(h).astype(BF16), w2_ref[k]) + b2_ref[k] for h, (j, k) in zip(hs, jk)]
    ck = {j: summ[2 * j] for j in range(nsq)}
    cv = {j: summ[2 * j + 1] for j in range(nsq)}

    tokpos = past + lax.broadcasted_iota(jnp.int32, (ls, 128), 0)
    blk = lax.broadcasted_iota(jnp.int32, (ls, 128), 1)
    cur = tokpos // SEL_LEN
    forced = (blk == 0) | (blk == cur) | (blk == cur - 1)
    sl_g = lambda g: slice(g * NSA_DH, (g + 1) * NSA_DH)
    qg = {u: (q_ref[u[0], u[1]] * NSA_SCALE).astype(BF16) for u in units}

    lg = {u: _dot_nt(qg[u], ck[u[0]][:, sl_g(u[1])].astype(BF16)) for u in units}
    pb = {u: _softmax_rows(lg[u] + bcmp_ref[u[1]]).astype(BF16) for u in units}
    o_cmp = {u: _dot(pb[u], cv[u[0]][:, sl_g(u[1])].astype(BF16)) for u in units}
    ps = {u: _dot(pb[u], ovl_ref[...]) for u in units}
    mem = {}
    for u in units:
        score = ps[u][0:ls]
        for r in range(1, NSA_HPG):
            score = score + ps[u][r * ls:(r + 1) * ls]
        sc = jnp.where(forced, jnp.inf, jnp.where(blk <= cur, score, -jnp.inf))
        member = _topk_member_lanes(sc, n_sel, n_top)
        mem[u] = jnp.concatenate([member] * NSA_HPG, axis=0).astype(BF16)
    addmask = {u: (_dot(mem[u], esel_ref[...]) - 1.0) * (-NEG) for u in units}

    lg_s = {u: _dot(qg[u], xb[u[0]][2 * gw + u[1] * NSA_DH:2 * gw + (u[1] + 1) * NSA_DH, :].astype(BF16)) for u in units}
    lg_w = {u: _dot(qg[u], wbuf[u[0], u[1] * NSA_DH:(u[1] + 1) * NSA_DH, :].astype(BF16)) for u in units}
    p_s = {u: _softmax_rows(lg_s[u] + bsel_ref[u[1]] + addmask[u]).astype(BF16) for u in units}
    p_w = {u: _softmax_rows(lg_w[u] + bwin_ref[u[1]]).astype(BF16) for u in units}
    o_sel = {u: _dot_nt(p_s[u], xb[u[0]][3 * gw + u[1] * NSA_DH:3 * gw + (u[1] + 1) * NSA_DH, :].astype(BF16))
             for u in units}
    o_win = {u: _dot_nt(p_w[u], wbuf[u[0], gw + u[1] * NSA_DH:gw + (u[1] + 1) * NSA_DH, :].astype(BF16)) for u in units}
    for u in units:
        gt = jax.nn.sigmoid(gate_ref[u[0], u[1]])
        o_ref[u[0], u[1]] = gt[:, 0:1] * o_cmp[u] + gt[:, 1:2] * o_sel[u] + gt[:, 2:3] * o_win[u]


def _decode_call(cache_t, page_table, q_s, kvn_t, winn_t, winbuf_t, gates, cw, tabs, new_win_t, *, l, past, n_sel):
    n_seq, n_pages = page_table.shape
    page = cache_t.shape[3]
    ls = kvn_t.shape[2]
    rows = NSA_HPG * ls
    w_len = winbuf_t.shape[3]
    t_cols = tabs["sel"].shape[-1]
    w_cols = tabs["win"].shape[-1]
    w1, b1, w2, b2 = cw
    gw = NSA_GROUPS * NSA_DH
    nsq = DEC_SEQ_BLOCK
    assert n_seq % nsq == 0
    n_top = min(SEL_TOP, n_sel)
    kern = functools.partial(_decode_kernel, l=l, n_pages=n_pages, page=page, ls=ls, past=past,
                             n_sel=n_sel, n_top=n_top)
    whole = lambda shape: pl.BlockSpec(shape, lambda s, pt: (0,) * len(shape))
    lay = lambda a: pl.BlockSpec((None,) + a.shape[1:], lambda s, pt: (l, 0, 0, 0))
    grid_spec = pltpu.PrefetchScalarGridSpec(
        num_scalar_prefetch=1,
        grid=(n_seq // nsq,),
        in_specs=[
            pl.BlockSpec(memory_space=pl.ANY),
            pl.BlockSpec((nsq, NSA_GROUPS, rows, NSA_DH), lambda s, pt: (s, 0, 0, 0)),
            pl.BlockSpec((nsq, 4 * gw, ls), lambda s, pt: (s, 0, 0)),
            pl.BlockSpec((nsq, 2 * gw, ls), lambda s, pt: (s, 0, 0)),
            pl.BlockSpec((None, nsq, 2 * gw, w_len), lambda s, pt: (l, s, 0, 0)),
            pl.BlockSpec((nsq, NSA_GROUPS, rows, 3), lambda s, pt: (s, 0, 0, 0)),
            lay(w1), lay(b1), lay(w2), lay(b2),
            whole(tabs["cmp"].shape), whole(tabs["sel"].shape), whole(tabs["win"].shape),
            whole(tabs["ovl"].shape), whole(tabs["esel"].shape),
            pl.BlockSpec(memory_space=pl.ANY),
        ],
        out_specs=[
            pl.BlockSpec((nsq, NSA_GROUPS, rows, NSA_DH), lambda s, pt: (s, 0, 0, 0)),
            pl.BlockSpec((None, nsq, 2 * gw, w_len), lambda s, pt: (l, s, 0, 0)),
        ],
        scratch_shapes=[pltpu.VMEM((2, nsq, 4 * gw, t_cols), F32),
                        pltpu.VMEM((nsq, 2 * gw, w_cols), F32),
                        pltpu.VMEM((nsq, 2, past, gw), F32),
                        pltpu.SemaphoreType.DMA((2,))],
    )
    return pl.pallas_call(
        kern,
        grid_spec=grid_spec,
        out_shape=[jax.ShapeDtypeStruct((n_seq, NSA_GROUPS, rows, NSA_DH), F32),
                   jax.ShapeDtypeStruct(new_win_t.shape, F32)],
        input_output_aliases={16: 1},
        compiler_params=pltpu.CompilerParams(dimension_semantics=("arbitrary",),
                                             vmem_limit_bytes=VMEM_LIMIT_BYTES),
        name="nsa_decode",
    )(page_table, cache_t, q_s, kvn_t, winn_t, winbuf_t, gates, w1, b1, w2, b2,
      tabs["cmp"], tabs["sel"], tabs["win"], tabs["ovl"], tabs["esel"], new_win_t)


def _decode_tables(rel_table, *, past, ls, n_cmp, n_sel, w_len):
    rows = NSA_HPG * ls
    t_cols = -(-(n_sel * SEL_LEN) // 128) * 128
    w_cols = -(-(w_len + ls) // 128) * 128
    n_sub = past // CMP_STRIDE
    tok = np.tile(np.arange(ls), NSA_HPG)
    hd = np.repeat(np.arange(NSA_HPG), ls)
    a = np.concatenate([past + tok, past + tok])
    head = np.concatenate([hd, NSA_HPG + hd])
    bc = np.where(np.arange(n_sub) < n_cmp, np.arange(n_sub) * CMP_STRIDE + CMP_LEN - 1, BIG)
    tabs = {
        "cmp": _bias_table(rel_table, a, bc, head, BIG).reshape(NSA_GROUPS, rows, n_sub),
        "sel": _bias_table(rel_table, a, np.arange(t_cols), head, BIG).reshape(NSA_GROUPS, rows, t_cols),
        "win": _bias_table(rel_table, a, past - w_len + np.arange(w_cols), head, WINDOW
                           ).reshape(NSA_GROUPS, rows, w_cols),
    }
    ovl = np.zeros((n_sub, 128), np.float32)
    ovl[:n_cmp, :n_sel] = _block_overlap(n_cmp, n_sel)
    esel = np.zeros((128, t_cols), np.float32)
    esel[np.arange(n_sel * SEL_LEN) // SEL_LEN, np.arange(n_sel * SEL_LEN)] = 1.0
    tabs["ovl"] = jnp.asarray(ovl, BF16)
    tabs["esel"] = jnp.asarray(esel, BF16)
    return tabs


TQ = 128
TKS = 256
SEL_OFFSETS = 3
WIN_OFFSETS = WINDOW // TQ + 2
AUG_ROWS = 32
V_ROWS = NSA_DH + 16


def _prompt_tables(rel_table, *, ln):
    nq = ln // TQ
    n_sub = ln // CMP_STRIDE
    n_cmp = (ln - CMP_LEN) // CMP_STRIDE + 1
    n_sel = -(-ln // SEL_LEN)
    qq = np.tile(np.arange(TQ), NSA_HPG)
    hd = np.repeat(np.arange(NSA_HPG), TQ)
    rows = NSA_HPG * TQ
    shift = TQ // CMP_STRIDE
    width = -(-(n_sub + (nq - 1) * shift) // 256) * 256
    cmp_g = []
    for g in range(NSA_GROUPS):
        master = _bias_table_cols(rel_table, -CMP_STRIDE * (np.arange(width) - (nq - 1) * shift),
                                  (CMP_LEN - 1) - qq, NSA_HPG * g + hd, jnp.zeros((rows,), F32), BIG)
        tiles = jnp.stack([master[(nq - 1 - i) * shift:(nq - 1 - i) * shift + n_sub] for i in range(nq)])
        cmp_g.append(jnp.where((np.arange(n_sub) < n_cmp)[:, None], tiles, NEG))
    cmp_t = jnp.stack(cmp_g, axis=1)
    far = rel_table[REL_BUCKETS - 1]
    far_hi = far.astype(BF16)
    far_lo = (far - far_hi.astype(F32)).astype(BF16)
    far_eff = far_hi.astype(F32) + far_lo.astype(F32)
    sel_t, win_t, crows = [], [], []
    for g in range(NSA_GROUPS):
        head = NSA_HPG * g + hd
        a = np.concatenate([TQ * w - np.arange(TKS) for w in range(WIN_OFFSETS)])
        sub = jnp.take(far_eff, jnp.asarray(head, jnp.int32))
        near = _bias_table_cols(rel_table, a[:SEL_OFFSETS * TKS], -qq, head, sub, BIG).reshape(SEL_OFFSETS, TKS, rows)
        sel_t.append(jnp.concatenate([near, jnp.zeros((1, TKS, rows), F32)], axis=0))
        win_t.append(_bias_table_cols(rel_table, a, -qq, head, jnp.zeros((rows,), F32), WINDOW
                                      ).reshape(WIN_OFFSETS, TKS, rows))
        cr = jnp.zeros((AUG_ROWS, rows), F32)
        cr = cr.at[0].set(jnp.take(far_hi.astype(F32), jnp.asarray(head, jnp.int32)))
        cr = cr.at[1].set(jnp.take(far_lo.astype(F32), jnp.asarray(head, jnp.int32)))
        crows.append(cr)
    nb = -(-n_sel // 8) * 8
    assert nb <= AUG_ROWS
    ovl_t = np.zeros((nb, n_sub), np.float32)
    ovl_t[:n_sel, :n_cmp] = _block_overlap(n_cmp, n_sel).T
    return {"cmp": cmp_t, "sel": jnp.stack(sel_t), "win": jnp.stack(win_t), "crows": jnp.stack(crows),
            "ovl_t": jnp.asarray(ovl_t, BF16), "n_sel": n_sel}


def _pcompress_kernel(k_ref, v_ref, w1_ref, b1_ref, w2_ref, b2_ref, ck_ref, cvt_ref, *, n_sub):
    ck, cv = [_compress_rows(lambda j, src=src: src[pl.ds(j, n_sub, stride=CMP_STRIDE), :],
                             n_sub, w1_ref[k], b1_ref[k], w2_ref[k], b2_ref[k])
              for k, src in ((0, k_ref), (1, v_ref))]
    ck_ref[...] = ck
    cvt_ref[...] = cv.T


def _pcompress_call(p, cw, *, l, nb, ln):
    w1, b1, w2, b2 = cw
    n_sub = ln // CMP_STRIDE
    gw = NSA_GROUPS * NSA_DH
    lay = lambda a: pl.BlockSpec((None,) + a.shape[1:], lambda b: (l, 0, 0, 0))
    out = pl.BlockSpec((None, n_sub, gw), lambda b: (b, 0, 0))
    return pl.pallas_call(
        functools.partial(_pcompress_kernel, n_sub=n_sub),
        grid=(nb,),
        in_specs=[pl.BlockSpec((ln, gw), lambda b: (b, P_KV4 // gw)),
                  pl.BlockSpec((ln, gw), lambda b: (b, P_KV4 // gw + 1)),
                  lay(w1), lay(b1), lay(w2), lay(b2)],
        out_specs=[out, pl.BlockSpec((None, gw, n_sub), lambda b: (b, 0, 0))],
        out_shape=[jax.ShapeDtypeStruct((nb, n_sub, gw), F32), jax.ShapeDtypeStruct((nb, gw, n_sub), F32)],
        compiler_params=pltpu.CompilerParams(dimension_semantics=("parallel",),
                                             vmem_limit_bytes=VMEM_LIMIT_BYTES),
        name="nsa_compress",
    )(p, p, w1, b1, w2, b2)


def _group_queries_t(q_t, g):
    hs = [q_t[(NSA_HPG * g + r) * NSA_DH:(NSA_HPG * g + r + 1) * NSA_DH, :] for r in range(NSA_HPG)]
    return jnp.concatenate(hs, axis=1).astype(F32) * NSA_SCALE


def _cmp_branch(i, q_t, gates_t, ck_ref, cvt_ref, bias_ref, ovl_ref, n_sel, n_top):
    nbk = ovl_ref.shape[0]
    blk = lax.broadcasted_iota(jnp.int32, (nbk, TQ), 0)
    cur = (i * TQ + lax.broadcasted_iota(jnp.int32, (nbk, TQ), 1)) // SEL_LEN
    forced = (blk == 0) | (blk == cur) | (blk == cur - 1)
    groups = range(NSA_GROUPS)
    gsl = [slice(g * NSA_DH, (g + 1) * NSA_DH) for g in groups]
    lg = [_dot(ck_ref[:, gsl[g]].astype(BF16), _group_queries_t(q_t, g).astype(BF16)) + bias_ref[g]
          for g in groups]
    pbs = []
    for g in groups:
        m = jnp.max(lg[g], axis=0, keepdims=True)
        e = jnp.exp(lg[g] - m)
        s = jnp.sum(e, axis=0, keepdims=True)
        pbs.append(jnp.where(m > 0.5 * NEG, e / s, 0.0).astype(BF16))
    o_t = [_dot(cvt_ref[gsl[g], :].astype(BF16), pbs[g]) for g in groups]
    scores = [[_dot(ovl_ref[...], pbs[g][:, r * TQ:(r + 1) * TQ]) for r in range(NSA_HPG)] for g in groups]
    pieces, members = [], []
    for g in groups:
        score_t = functools.reduce(lambda a, b: a + b, scores[g])
        sc = jnp.where(forced, jnp.inf, jnp.where(blk <= cur, score_t, -jnp.inf))
        member_t = _topk_member_rows(sc, n_sel, n_top)
        if nbk < AUG_ROWS:
            member_t = jnp.concatenate([member_t, jnp.zeros((AUG_ROWS - nbk, TQ), F32)], axis=0)
        members.append(member_t)
        for r in range(NSA_HPG):
            h = NSA_HPG * g + r
            pieces.append(gates_t[h:h + 1] * o_t[g][:, r * TQ:(r + 1) * TQ])
    return pieces, members


def _tile_state(lg_t, v_aug_t, valid):
    m = jnp.max(lg_t, axis=0, keepdims=True)
    acc = _dot(v_aug_t, jnp.exp(lg_t - m).astype(BF16))
    if valid is not None:
        m = jnp.where(valid, m, NEG)
    return m, acc


def _merge_states(states):
    m = functools.reduce(jnp.maximum, [s[0] for s in states])
    acc = states[0][1] * jnp.exp(states[0][0] - m)
    for s in states[1:]:
        acc = acc + s[1] * jnp.exp(s[0] - m)
    return m, acc


def _pattn_kernel(q_ref, kv_ref, wkv_ref, wing_ref, ck_ref, cvt_ref, bcmp_ref, ovl_ref, bsel_ref, bwin_ref,
                  crow_ref, _, o_ref, kvt_ref, kk_scr, vs_scr, vw_scr, *, n_sel, n_top):
    i = pl.program_id(1)
    gw = NSA_GROUPS * NSA_DH
    cols = NSA_HPG * TQ
    ln = kv_ref.shape[0]

    @pl.when(i == 0)
    def _():
        pos = lax.broadcasted_iota(jnp.int32, (ln, NSA_DH), 0)
        lane = lax.broadcasted_iota(jnp.int32, (ln, NSA_DH), 1)
        extra = ((lane == pos // SEL_LEN) | (lane == AUG_ROWS) | (lane == AUG_ROWS + 1)).astype(F32)
        ones_row = (lax.broadcasted_iota(jnp.int32, (V_ROWS - NSA_DH, ln), 0) == 0).astype(F32)
        zeros = jnp.zeros((ln, NSA_DH), F32)
        for kind in range(4):
            kvt_ref[kind * gw:(kind + 1) * gw, :] = kv_ref[:, kind * gw:(kind + 1) * gw].T
        vs_t = kvt_ref[3 * gw:4 * gw, :]
        vw_t = wkv_ref[:, gw:2 * gw].T
        for g in range(NSA_GROUPS):
            lo, hi = g * NSA_DH, (g + 1) * NSA_DH
            kk_scr[g] = jnp.concatenate([kv_ref[:, 2 * gw + lo:2 * gw + hi], extra], axis=1).astype(BF16)
            kk_scr[NSA_GROUPS + g] = jnp.concatenate([wkv_ref[:, lo:hi], zeros], axis=1).astype(BF16)
            vs_scr[g] = jnp.concatenate([vs_t[lo:hi], ones_row], axis=0).astype(BF16)
            vw_scr[g] = jnp.concatenate([vw_t[lo:hi], ones_row], axis=0).astype(BF16)

    gates_t = jax.nn.sigmoid(wing_ref[:, 2 * gw:3 * gw]).T
    q_raw = q_ref[...]
    q_t = q_raw.astype(F32) * NSA_SCALE
    m_tile = i // (TKS // TQ)

    def augmented(g, mask_t):
        q_aug = jnp.concatenate(
            [jnp.concatenate([q_t[(NSA_HPG * g + r) * NSA_DH:(NSA_HPG * g + r + 1) * NSA_DH], mask_t], axis=0)
             for r in range(NSA_HPG)], axis=1)
        return jnp.concatenate([q_aug, crow_ref[g]], axis=0).astype(BF16)

    def tile_states(tiles, q_augs):
        lgs = []
        for branch, kt, bias_ref, _ in tiles:
            k0 = pl.multiple_of(kt * TKS, TKS)
            lg = [_dot(kk_scr[branch * NSA_GROUPS + g, pl.ds(k0, TKS), :], q_augs[g]) for g in range(NSA_GROUPS)]
            if bias_ref is not None:
                w = i - kt * (TKS // TQ)
                w = jnp.minimum(w, SEL_OFFSETS) if branch == 0 else w
                lg = [lg[g] + bias_ref[g, w] for g in range(NSA_GROUPS)]
            lgs.append(lg)
        out = []
        for (branch, kt, _, valid), lg in zip(tiles, lgs):
            k0 = pl.multiple_of(kt * TKS, TKS)
            v_scr = vs_scr if branch == 0 else vw_scr
            out.append([_tile_state(lg[g], v_scr[g, :, pl.ds(k0, TKS)], valid) for g in range(NSA_GROUPS)])
        return out

    t_m1, t_m2 = jnp.maximum(m_tile - 1, 0), jnp.maximum(m_tile - 2, 0)
    no_mask = jnp.zeros((AUG_ROWS, TQ), F32)
    wst = tile_states([(1, t_m2, bwin_ref, m_tile >= 2), (1, t_m1, bwin_ref, m_tile >= 1),
                       (1, m_tile, bwin_ref, None)], [augmented(g, no_mask) for g in range(NSA_GROUPS)])
    win = [_merge_states([wst[0][g], wst[1][g], wst[2][g]]) for g in range(NSA_GROUPS)]
    cmp_pieces, members = _cmp_branch(i, q_raw, gates_t, ck_ref, cvt_ref, bcmp_ref, ovl_ref, n_sel, n_top)
    q_augs = [augmented(g, (members[g] - 1.0) * (-NEG)) for g in range(NSA_GROUPS)]

    n_far = jnp.maximum(m_tile - 1, 0)

    def far_body(n, carry):
        t1 = 2 * n + 1
        sts = tile_states([(0, 2 * n, None, None), (0, jnp.minimum(t1, n_far - 1), None, t1 < n_far)], q_augs)
        return tuple(_merge_states([carry[g], sts[0][g], sts[1][g]]) for g in range(NSA_GROUPS))

    init1 = (jnp.full((1, cols), NEG, F32), jnp.zeros((V_ROWS, cols), F32))
    far = lax.fori_loop(0, (n_far + 1) // 2, far_body, (init1,) * NSA_GROUPS)
    sst = tile_states([(0, t_m1, bsel_ref, m_tile >= 1), (0, m_tile, bsel_ref, None)], q_augs)
    sel = [_merge_states([far[g], sst[0][g], sst[1][g]]) for g in range(NSA_GROUPS)]
    pieces = []
    for g in range(NSA_GROUPS):
        acc_s, acc_w = sel[g][1], win[g][1]
        o_s = acc_s[0:NSA_DH] / acc_s[NSA_DH:NSA_DH + 1]
        o_w = acc_w[0:NSA_DH] / acc_w[NSA_DH:NSA_DH + 1]
        for r in range(NSA_HPG):
            h = NSA_HPG * g + r
            cs = slice(r * TQ, (r + 1) * TQ)
            pieces.append(gates_t[NSA_HEADS + h:NSA_HEADS + h + 1] * o_s[:, cs]
                          + gates_t[2 * NSA_HEADS + h:2 * NSA_HEADS + h + 1] * o_w[:, cs])
    o_ref[...] = (jnp.concatenate(cmp_pieces, axis=0) + jnp.concatenate(pieces, axis=0)).T


def _pattn_call(p, q_t, ck, cvt, tabs, kv_acc, *, l, nb, ln):
    nq = ln // TQ
    n_sub = ln // CMP_STRIDE
    n_sel = tabs["n_sel"]
    gw = NSA_GROUPS * NSA_DH
    tile = lambda w, cb: pl.BlockSpec((TQ, w), lambda b, i, cb=cb: (b * nq + i, cb))
    tile_t = pl.BlockSpec((None, NSA_WIDTH, TQ), lambda b, i: (b, 0, i))
    seq = lambda cb: pl.BlockSpec((ln, 4 * gw), lambda b, i, cb=cb: (b, cb))
    whole = lambda a: pl.BlockSpec(a.shape, lambda b, i: (0,) * a.ndim, pipeline_mode=pl.Buffered(1))
    k_scr = pltpu.VMEM((2 * NSA_GROUPS, ln, 2 * NSA_DH), BF16)
    v_scr = pltpu.VMEM((NSA_GROUPS, V_ROWS, ln), BF16)
    return pl.pallas_call(
        functools.partial(_pattn_kernel, n_sel=n_sel, n_top=min(SEL_TOP, n_sel)),
        grid=(nb, nq),
        in_specs=[tile_t, seq(P_KV4 // (4 * gw)), seq(P_WING // (4 * gw)),
                  tile(4 * gw, P_WING // (4 * gw)),
                  pl.BlockSpec((None, n_sub, gw), lambda b, i: (b, 0, 0)),
                  pl.BlockSpec((None, gw, n_sub), lambda b, i: (b, 0, 0)),
                  pl.BlockSpec((None, NSA_GROUPS, n_sub, NSA_HPG * TQ), lambda b, i: (i, 0, 0, 0)),
                  whole(tabs["ovl_t"]),
                  whole(tabs["sel"]), whole(tabs["win"]), whole(tabs["crows"]),
                  pl.BlockSpec(memory_space=pl.ANY)],
        out_specs=[tile(NSA_WIDTH, 0),
                   pl.BlockSpec((None, None, 4 * gw, ln), lambda b, i: (l, b, 0, 0))],
        out_shape=[jax.ShapeDtypeStruct((nb * ln, NSA_WIDTH), F32),
                   jax.ShapeDtypeStruct(kv_acc.shape, F32)],
        input_output_aliases={11: 1},
        scratch_shapes=[k_scr, v_scr, v_scr],
        compiler_params=pltpu.CompilerParams(dimension_semantics=("parallel", "arbitrary"),
                                             vmem_limit_bytes=VMEM_LIMIT_BYTES),
        name="nsa_attention",
    )(q_t, p, p, p, ck, cvt, tabs["cmp"], tabs["ovl_t"], tabs["sel"], tabs["win"], tabs["crows"], kv_acc)


HG_SUB = 8


def _split3(x):
    h = x.astype(BF16)
    r = x - h.astype(F32)
    m = r.astype(BF16)
    return h, m, (r - m.astype(F32)).astype(BF16)


def _row_bcast(x, row, n):
    return jnp.broadcast_to(x[row:row + 1, :], (n, x.shape[1]))


def _hgrn_prompt_kernel(q_ref, f_ref, i_ref, g_ref, lb_ref, gain_ref, tri_ref, ecat_ref, o_ref, s_ref,
                        st_scr, *, n_chunks):
    step = pl.program_id(1)
    c = HG_CHUNK
    n_sub = c // HG_SUB

    @pl.when(step == 0)
    def _():
        st_scr[...] = jnp.zeros_like(st_scr)

    lb = lb_ref[...]
    rowi = lax.broadcasted_iota(jnp.int32, (c, HG_DK), 0)
    rmod = rowi % HG_SUB
    rsub = rowi // HG_SUB
    ti = lax.broadcasted_iota(jnp.int32, (c, c), 0)
    si = lax.broadcasted_iota(jnp.int32, (c, c), 1)
    diag_mask = (ti // HG_SUB == si // HG_SUB)

    for ch in range(n_chunks):
        r0 = ch * c
        fz = f_ref[pl.ds(r0, c), :]
        sg = jax.nn.sigmoid(fz)
        logf = jnp.log(lb + (1.0 - lb) * sg)
        kk = (1.0 - lb) * jax.nn.sigmoid(-fz)
        qq = _silu(q_ref[pl.ds(r0, c), :])
        vv = i_ref[pl.ds(r0, c), :]
        gg = g_ref[pl.ds(r0, c), :]
        h0, h1, h2 = _split3(logf)
        tri = tri_ref[...]
        b_all = _dot(tri, h0) + _dot(tri, h1) + _dot(tri, h2)
        heads = range(HG_HEADS)
        hsl = [slice(h * HG_DK, (h + 1) * HG_DK) for h in heads]
        bs_ = [b_all[:, s] for s in hsl]
        qs = [qq[:, s] for s in hsl]
        ks_ = [kk[:, s] for s in hsl]
        vbs = [vv[:, s].astype(BF16) for s in hsl]
        sts = [st_scr[h] for h in heads]
        qes, lhss, rhss, zcats, kdecs, elast = [], [], [], [], [], []
        for h in heads:
            b, q, k = bs_[h], qs[h], ks_[h]
            qes.append((q * jnp.exp(b)).astype(BF16))
            bstart = jnp.concatenate(
                [jnp.zeros((HG_SUB, HG_DK), F32)]
                + [_row_bcast(b, HG_SUB * i - 1, HG_SUB) for i in range(1, n_sub)], axis=0)
            bend = jnp.concatenate([_row_bcast(b, HG_SUB * (j + 1) - 1, HG_SUB) for j in range(n_sub)], axis=0)
            qh = q * jnp.exp(b - bstart)
            kh = k * jnp.exp(bend - b)
            lhs, rhs = [], []
            for j in range(n_sub - 1):
                bj = b[HG_SUB * (j + 1) - 1:HG_SUB * (j + 1), :]
                dj = jnp.exp(jnp.minimum(bstart - bj, 0.0))
                lhs.append(jnp.where(rsub > j, qh * dj, 0.0).astype(BF16))
                rhs.append(jnp.where(rsub == j, kh, 0.0).astype(BF16))
            lhss.append(jnp.concatenate(lhs, axis=1))
            rhss.append(jnp.concatenate(rhs, axis=1))
            zs = []
            for s in range(HG_SUB):
                ksb = jnp.concatenate([_row_bcast(k, HG_SUB * i + s, HG_SUB) for i in range(n_sub)], axis=0)
                bsb = jnp.concatenate([_row_bcast(b, HG_SUB * i + s, HG_SUB) for i in range(n_sub)], axis=0)
                dec = jnp.where(rmod >= s, jnp.exp(b - bsb), 0.0)
                zs.append((q * ksb * dec).astype(BF16))
            zcats.append(jnp.concatenate(zs, axis=1))
            blast = b[c - 1:c, :]
            kdecs.append((k * jnp.exp(blast - b)).astype(BF16))
            elast.append(jnp.exp(blast))
        o_in = [_dot_nt(qes[h], sts[h].astype(BF16)) for h in heads]
        a_off = [_dot_nt(lhss[h], rhss[h]) for h in heads]
        a_diag = [_dot(zcats[h], ecat_ref[...]) for h in heads]
        upd = [lax.dot_general(vbs[h], kdecs[h], (((0,), (0,)), ((), ())), preferred_element_type=F32)
               for h in heads]
        for h in heads:
            st_scr[h] = sts[h] * elast[h] + upd[h]
        a = [(a_off[h] + jnp.where(diag_mask, a_diag[h], 0.0)).astype(BF16) for h in heads]
        o_intra = [_dot(a[h], vbs[h]) for h in heads]
        outs = []
        for h in heads:
            o = o_in[h] + o_intra[h]
            y = o * lax.rsqrt(jnp.mean(o * o, axis=-1, keepdims=True) + EPS) * gain_ref[...]
            outs.append(y * _silu(gg[:, hsl[h]]))
        o_ref[pl.ds(r0, c), :] = jnp.concatenate(outs, axis=1)

    @pl.when(step == pl.num_programs(1) - 1)
    def _():
        for h in range(HG_HEADS):
            s_ref[h] = st_scr[h].T


def _hgrn_prompt_call(p, lb_l, gain_l, *, nb, ln, tr):
    c = HG_CHUNK
    w = HG_HEADS * HG_DK
    steps = ln // tr
    tri = jnp.asarray(np.tril(np.ones((c, c), np.float32)), BF16)
    ecat = np.zeros((HG_SUB * HG_DK, c), np.float32)
    for s in range(HG_SUB):
        ecat[s * HG_DK:(s + 1) * HG_DK, s::HG_SUB] = 1.0
    col = lambda off: pl.BlockSpec((tr, w), lambda b, i, off=off: (b * steps + i, off // w))
    return pl.pallas_call(
        functools.partial(_hgrn_prompt_kernel, n_chunks=tr // c),
        grid=(nb, steps),
        in_specs=[col(P_QB), col(P_FB), col(P_IB), col(P_GB),
                  pl.BlockSpec((1, w), lambda b, i: (0, 0)),
                  pl.BlockSpec((1, HG_DV), lambda b, i: (0, 0)),
                  pl.BlockSpec((c, c), lambda b, i: (0, 0)),
                  pl.BlockSpec((HG_SUB * HG_DK, c), lambda b, i: (0, 0))],
        out_specs=[pl.BlockSpec((tr, w), lambda b, i: (b * steps + i, 0)),
                   pl.BlockSpec((None, HG_HEADS, HG_DK, HG_DV), lambda b, i: (b, 0, 0, 0))],
        out_shape=[jax.ShapeDtypeStruct((nb * ln, w), F32),
                   jax.ShapeDtypeStruct((nb, HG_HEADS, HG_DK, HG_DV), F32)],
        scratch_shapes=[pltpu.VMEM((HG_HEADS, HG_DV, HG_DK), F32)],
        compiler_params=pltpu.CompilerParams(dimension_semantics=("parallel", "arbitrary"),
                                             vmem_limit_bytes=VMEM_LIMIT_BYTES),
        name="hgrn_prompt",
    )(p, p, p, p, lb_l.reshape(1, w), gain_l.reshape(1, HG_DV), tri, jnp.asarray(ecat, BF16))


HG_SEQ_BLOCK = 8


def _hgrn_decode_kernel(q_ref, f_ref, i_ref, g_ref, lb_ref, gain_ref, s0_ref, _, o_ref, s_ref, *, ls):
    lb = lb_ref[...]
    rowi = lax.broadcasted_iota(jnp.int32, (8, HG_DK), 0)
    zpad = jnp.zeros((8 - ls, HG_DK), F32)
    for n in range(q_ref.shape[0]):
        fz = f_ref[n]
        logf = jnp.log(lb + (1.0 - lb) * jax.nn.sigmoid(fz))
        kk = (1.0 - lb) * jax.nn.sigmoid(-fz)
        qq = _silu(q_ref[n])
        vv = i_ref[n]
        gg = g_ref[n]
        brows = [logf[0:1]]
        for t in range(1, ls):
            brows.append(brows[-1] + logf[t:t + 1])
        b_all = jnp.concatenate(brows, axis=0)
        outs = []
        for h in range(HG_HEADS):
            sl = slice(h * HG_DK, (h + 1) * HG_DK)
            b, q, k, v = b_all[:, sl], qq[:, sl], kk[:, sl], vv[:, sl]
            st = s0_ref[n, h]
            blast = b[ls - 1:ls]
            e_last = jnp.exp(blast)
            e_hi = e_last.astype(BF16)
            e_lo = (e_last - e_hi.astype(F32)).astype(BF16)
            lhs = jnp.concatenate([(k * jnp.exp(blast - b)).astype(BF16), e_hi, e_lo,
                                   jnp.zeros((8 - ls - 2, HG_DK), BF16)], axis=0)
            ones2 = ((rowi >= ls) & (rowi < ls + 2)).astype(F32)
            rhs = jnp.concatenate([jnp.concatenate([v, zpad], axis=0), ones2], axis=1).astype(BF16)
            upd = lax.dot_general(lhs, rhs, (((0,), (0,)), ((), ())), preferred_element_type=F32)
            s_ref[n, h] = upd[:, HG_DV:] * st + upd[:, :HG_DV]
            o = _dot(jnp.concatenate([q * jnp.exp(b), zpad], axis=0).astype(BF16), st.astype(BF16))[0:ls]
            for s in range(ls):
                keep = rowi[0:ls] >= s
                dec = jnp.where(keep, jnp.exp(jnp.minimum(b - b[s:s + 1], 0.0)), 0.0)
                a_s = jnp.sum(q * k[s:s + 1] * dec, axis=-1, keepdims=True)
                o = o + a_s * v[s:s + 1]
            y = o * lax.rsqrt(jnp.mean(o * o, axis=-1, keepdims=True) + EPS) * gain_ref[...]
            outs.append(y * _silu(gg[:, sl]))
        o_ref[n] = jnp.concatenate(outs, axis=1)


def _hgrn_decode_call(qb, fb, ib, gb, lb_l, gain_l, state, new_state, *, l):
    n_seq, ls, w = qb.shape
    sb = HG_SEQ_BLOCK
    assert ls + 2 <= 8 and n_seq % sb == 0
    tok = pl.BlockSpec((sb, ls, w), lambda s: (s, 0, 0))
    layer = pl.BlockSpec((None, sb, HG_HEADS, HG_DK, HG_DV), lambda s: (l, s, 0, 0, 0))
    return pl.pallas_call(
        functools.partial(_hgrn_decode_kernel, ls=ls),
        grid=(n_seq // sb,),
        in_specs=[tok, tok, tok, tok,
                  pl.BlockSpec((1, w), lambda s: (0, 0)),
                  pl.BlockSpec((1, HG_DV), lambda s: (0, 0)),
                  layer, pl.BlockSpec(memory_space=pl.ANY)],
        out_specs=[tok, layer],
        out_shape=[jax.ShapeDtypeStruct((n_seq, ls, w), F32),
                   jax.ShapeDtypeStruct(new_state.shape, F32)],
        input_output_aliases={7: 1},
        compiler_params=pltpu.CompilerParams(dimension_semantics=("parallel",)),
        name="hgrn_decode",
    )(qb, fb, ib, gb, lb_l.reshape(1, w), gain_l.reshape(1, HG_DV), state, new_state)


def kernel(x_prompt, x_sample, cache_nsa_kv, state_win_kv, state_hgrn, page_table, c_prompt, c_sample,
           norm_g, ada_w, ada_b, ffn_w_in, ffn_w_out, w_in, cmp_w1, cmp_b1, cmp_w2, cmp_b2, rel_table,
           hgrn_lb_logits, hgrn_norm_g, w_branch_nsa, w_branch_hgrn, w_out, final_g):
    depth = norm_g.shape[0]
    bp, lp, d = x_prompt.shape
    bs, ls, _ = x_sample.shape
    n_seq, n_pages = page_table.shape
    n_pool, page = cache_nsa_kv.shape[1:3]
    past = n_pages * page
    w_len = state_win_kv.shape[2]
    gw = NSA_GROUPS * NSA_DH
    assert bs == 128 and n_seq == bs, "sample rows are tiled per 128 sequences"
    assert lp % TKS == 0 and lp >= WINDOW and (lp // CMP_STRIDE) % 128 == 0
    n_cmp_s = (past + ls - CMP_LEN) // CMP_STRIDE + 1
    n_sel_s = -(-(past + ls) // SEL_LEN)
    assert (n_cmp_s - 1) * CMP_STRIDE + CMP_LEN <= past and (past // CMP_STRIDE) % 128 == 0
    assert past % SEL_LEN == 0 and ls <= 8 and past // SEL_LEN == (past + ls - 1) // SEL_LEN

    lb = jnp.cumsum(jax.nn.softmax(hgrn_lb_logits.astype(F32), axis=0), axis=0)
    lb = lb - lb[:1]

    w_in_p, wq_t, w_gate = _split_w_in(w_in)
    wpa_b = w_branch_nsa.astype(BF16)
    wpb_b = w_branch_hgrn.astype(BF16)
    wo_b = w_out.astype(BF16)
    cw = _compress_weights(cmp_w1, cmp_b1, cmp_w2, cmp_b2)
    tabs_p = _prompt_tables(rel_table, ln=lp)
    tabs_s = _decode_tables(rel_table, past=past, ls=ls, n_cmp=n_cmp_s, n_sel=n_sel_s, w_len=w_len)
    cache_t = cache_nsa_kv.reshape(depth, n_pool, page, 4 * gw).transpose(0, 1, 3, 2)
    winbuf_t = state_win_kv.reshape(depth, bs, w_len, 2 * gw).transpose(0, 1, 3, 2)

    mods = _mods_call(jnp.concatenate([c_sample, c_prompt], axis=0), ada_w, ada_b)

    def sublayers(x2, l, tm, seq_tiles, mixers):
        x2 = _ffn_call(x2, norm_g, mods, ffn_w_in, ffn_w_out, final_g, l=l, s=0, sub=0, tm=tm, tf=256,
                       seq_tiles=seq_tiles, final=False)
        tm_proj = tm if seq_tiles is None else tm * seq_tiles[1]
        p, q_t = _proj_call(x2, norm_g, mods, w_in_p, wq_t, l=l, tm=tm_proj, tn=512,
                            seq_tiles=None if seq_tiles is None else (seq_tiles[0], 1))
        o_a, o_b, extras = mixers(p, q_t)
        x2 = _merge_call(x2, o_a, o_b, norm_g, mods, w_gate, wpa_b, wpb_b, wo_b, l=l, tm=tm, seq_tiles=seq_tiles)
        x2 = _ffn_call(x2, norm_g, mods, ffn_w_in, ffn_w_out, final_g, l=l, s=1, sub=2, tm=tm, tf=256,
                       seq_tiles=seq_tiles, final=(l == depth - 1))
        return x2, extras

    tm_p = 1024 if lp % 1024 == 0 else lp
    x2 = x_prompt.reshape(bp * lp, d)
    win_p, hg_p = [], []
    kv_acc = jnp.zeros((depth, bp, 4 * gw, lp), F32)
    for l in range(depth):
        def prompt_mixers(p, q_t, kv_acc, l=l):
            ck, cvt = _pcompress_call(p, cw, l=l, nb=bp, ln=lp)
            o_a, kv_acc = _pattn_call(p, q_t, ck, cvt, tabs_p, kv_acc, l=l, nb=bp, ln=lp)
            o_b, s_new = _hgrn_prompt_call(p, lb[l], hgrn_norm_g[l], nb=bp, ln=lp, tr=512)
            p3 = p.reshape(bp, lp, P_TOTAL)
            nwin = p3[:, lp - min(WINDOW, lp):, P_WING:P_WING + 2 * gw].reshape(bp, -1, 2, NSA_GROUPS, NSA_DH)
            return o_a, o_b, (kv_acc, nwin, s_new)

        x2, (kv_acc, nwin, s_new) = sublayers(x2, l, tm_p, (bs, lp // tm_p),
                                              functools.partial(prompt_mixers, kv_acc=kv_acc))
        win_p.append(nwin)
        hg_p.append(s_new)
    y_p = x2.reshape(bp, lp, d)
    kv_p = kv_acc.transpose(0, 1, 3, 2).reshape(depth, bp, lp, 4, NSA_GROUPS, NSA_DH)

    x2 = x_sample.transpose(1, 0, 2).reshape(ls * bs, d)
    kv_s = []
    win_acc = jnp.zeros(winbuf_t.shape, F32)
    hg_acc = jnp.zeros(state_hgrn.shape, F32)
    for l in range(depth):
        def sample_mixers(p, q_t, win_acc, hg_acc, l=l):
            ps = p.reshape(ls, bs, P_TOTAL).transpose(1, 0, 2)
            q_s = q_t.astype(F32).reshape(NSA_GROUPS, NSA_HPG, NSA_DH, ls, bs)
            q_s = q_s.transpose(4, 0, 1, 3, 2).reshape(bs, NSA_GROUPS, NSA_HPG * ls, NSA_DH)
            ga = ps[:, :, P_WING + 2 * gw:P_WING + 2 * gw + 3 * NSA_HEADS].reshape(bs, ls, 3, NSA_GROUPS, NSA_HPG)
            ga = ga.transpose(0, 3, 4, 1, 2).reshape(bs, NSA_GROUPS, NSA_HPG * ls, 3)
            kvn = ps[:, :, P_KV4:P_KV4 + 4 * gw]
            winn = ps[:, :, P_WING:P_WING + 2 * gw]
            o, win_acc = _decode_call(cache_t, page_table, q_s, kvn.transpose(0, 2, 1), winn.transpose(0, 2, 1),
                                      winbuf_t, ga, cw, tabs_s, win_acc, l=l, past=past, n_sel=n_sel_s)
            o_a = o.reshape(bs, NSA_GROUPS, NSA_HPG, ls, NSA_DH).transpose(3, 0, 1, 2, 4).reshape(ls * bs, NSA_WIDTH)
            o_b, hg_acc = _hgrn_decode_call(ps[:, :, P_QB:P_QB + 512], ps[:, :, P_FB:P_FB + 512],
                                            ps[:, :, P_IB:P_IB + 512], ps[:, :, P_GB:P_GB + 512],
                                            lb[l], hgrn_norm_g[l], state_hgrn, hg_acc, l=l)
            o_b = o_b.transpose(1, 0, 2).reshape(ls * bs, HG_WIDTH)
            return o_a, o_b, (kvn.reshape(bs, ls, 4, NSA_GROUPS, NSA_DH), win_acc, hg_acc)

        x2, (kv4, win_acc, hg_acc) = sublayers(
            x2, l, ls * bs, None, functools.partial(sample_mixers, win_acc=win_acc, hg_acc=hg_acc))
        kv_s.append(kv4)
    y_s = x2.reshape(ls, bs, d).transpose(1, 0, 2)
    win_s = win_acc.transpose(0, 1, 3, 2).reshape(depth, bs, w_len, 2, NSA_GROUPS, NSA_DH)

    return (y_p, y_s, kv_p, jnp.stack(kv_s), jnp.stack(win_p), win_s, jnp.stack(hg_p), hg_acc)
```
